```python
import jax, jax.numpy as jnp
from jax import lax
import numpy as np

D_MODEL = 2048
BATCH = 4
SEQ = 4096
DEPTH = 4
DEC_BATCH = 32
DEC_SEQ = 32
PAST_LEN = 2048

CHUNK = 64
Q_BLOCK = 128
FOX_HEADS = 8
FOX_HEAD_DIM = 128
MLA_HEADS = 16
MLA_Q_LORA = 512
MLA_KV_LORA = 512
MLA_NOPE_DIM = 128
MLA_ROPE_DIM = 64
MLA_V_DIM = 128
ROPE_THETA = 10000.0
D_FF = 5632
NORM_EPS = 1e-6
NEG_INF = -1e30
FOX_WIDTH = FOX_HEADS * FOX_HEAD_DIM
MLA_WIDTH = MLA_HEADS * MLA_V_DIM
IN_SPLITS = (FOX_WIDTH, FOX_WIDTH, FOX_WIDTH, FOX_HEADS, MLA_Q_LORA, MLA_KV_LORA, MLA_ROPE_DIM, D_MODEL, D_MODEL)
D_IN = sum(IN_SPLITS)

kernel_name = "fox_mla_macaron_streaming_encoder"

F32 = jnp.float32


def rmsnorm(x, g):
    xf = x.astype(F32)
    y = xf * lax.rsqrt(jnp.mean(xf * xf, axis=-1, keepdims=True) + NORM_EPS)
    return (y * g.astype(F32)).astype(x.dtype)


def rope(x, pos):
    half = MLA_ROPE_DIM // 2
    inv = ROPE_THETA ** (-jnp.arange(half, dtype=F32) / half)
    ang = pos.astype(F32)[:, None] * inv[None, :]
    cos = jnp.cos(ang)[:, None, :]
    sin = jnp.sin(ang)[:, None, :]
    xf = x.astype(F32)
    x1, x2 = xf[..., :half], xf[..., half:]
    return jnp.concatenate([x1 * cos - x2 * sin, x1 * sin + x2 * cos], axis=-1).astype(x.dtype)


def swiglu_ffn(x, g, w_gate, w_up, w_down):
    h = rmsnorm(x, g)
    return (jax.nn.silu(h @ w_gate) * (h @ w_up)) @ w_down


def frame_causal(tq, tk):
    return tk <= tq


def chunk_causal(tq, tk):
    return (tk // CHUNK) <= (tq // CHUNK)


def attend(q_parts, k_parts, v, q_pos, k_pos, mask_fn, q_F=None, k_F=None):
    B, T = q_parts[0].shape[:2]
    scale = float(sum(q.shape[-1] for q in q_parts)) ** -0.5
    kF_t = None if k_F is None else jnp.swapaxes(k_F.astype(F32), 1, 2)

    def one_block(args):
        qb_parts, pb = args[0], args[1]
        s = 0.0
        for qb, k in zip(qb_parts, k_parts):
            eq = "bqhd,bkhd->bhqk" if k.ndim == 4 else "bqhd,bkd->bhqk"
            s = s + jnp.einsum(eq, qb, k).astype(F32)
        s = s * scale
        if kF_t is not None:
            fb = jnp.swapaxes(args[2].astype(F32), 1, 2)
            s = s + (fb[..., :, None] - kF_t[..., None, :])
        s = jnp.where(mask_fn(pb[:, None], k_pos[None, :]), s, NEG_INF)
        p = jax.nn.softmax(s, axis=-1)
        return jnp.einsum("bhqk,bkhd->bqhd", p.astype(v.dtype), v)

    tail = () if q_F is None else (q_F,)
    if T <= Q_BLOCK:
        return one_block((tuple(q_parts), q_pos) + tail)
    nb = T // Q_BLOCK
    blk = lambda a: jnp.swapaxes(a.reshape((B, nb, Q_BLOCK) + a.shape[2:]), 0, 1)
    xs = (tuple(blk(q) for q in q_parts), q_pos.reshape(nb, Q_BLOCK)) + tuple(blk(f) for f in tail)
    out = lax.map(one_block, xs)
    return jnp.swapaxes(out, 0, 1).reshape((B, T) + out.shape[3:])


def mixer(h, pos, past, p):
    B, T, _ = h.shape
    z = h @ p["w_in"]
    idx = [int(i) for i in np.cumsum(IN_SPLITS)[:-1]]
    q_a, k_a, v_a, f_a, c_q, c_kv, k_pe, g_a, g_b = jnp.split(z, idx, axis=-1)

    q_f = rmsnorm(q_a.reshape(B, T, FOX_HEADS, FOX_HEAD_DIM), p["fox_q_norm"])
    k_f = rmsnorm(k_a.reshape(B, T, FOX_HEADS, FOX_HEAD_DIM), p["fox_k_norm"])
    v_f = v_a.reshape(B, T, FOX_HEADS, FOX_HEAD_DIM)
    logf = jax.nn.log_sigmoid((f_a + p["b_forget"]).astype(F32))

    q_m = (rmsnorm(c_q, p["mla_q_lat_norm"]) @ p["w_uq"]).reshape(B, T, MLA_HEADS, MLA_NOPE_DIM + MLA_ROPE_DIM)
    q_nope = rmsnorm(q_m[..., :MLA_NOPE_DIM], p["mla_q_nope_norm"])
    q_pe = rope(rmsnorm(q_m[..., MLA_NOPE_DIM:], p["mla_q_rope_norm"]), pos)
    c_kv = rmsnorm(c_kv, p["mla_kv_lat_norm"])
    k_pe = rope(rmsnorm(k_pe, p["mla_k_rope_norm"])[:, :, None, :], pos)[:, :, 0, :]

    if past is None:
        kf_all, vf_all, logf_all, ckv_all, kpe_all, k_pos = k_f, v_f, logf, c_kv, k_pe, pos
    else:
        ck, cv, clf, cckv, ckpe = past
        cat = lambda a, b: jnp.concatenate([a, b], axis=1)
        kf_all, vf_all = cat(ck, k_f), cat(cv, v_f)
        logf_all = cat(clf.astype(F32), logf)
        ckv_all, kpe_all = cat(cckv, c_kv), cat(ckpe, k_pe)
        k_pos = jnp.arange(ck.shape[1] + T, dtype=jnp.int32)

    F_all = jnp.cumsum(logf_all, axis=1)
    o_f = attend((q_f,), (kf_all,), vf_all, pos, k_pos, frame_causal, F_all[:, -T:], F_all)

    S = ckv_all.shape[1]
    kv = (ckv_all @ p["w_ukv"]).reshape(B, S, MLA_HEADS, MLA_NOPE_DIM + MLA_V_DIM)
    k_nope = rmsnorm(kv[..., :MLA_NOPE_DIM], p["mla_k_nope_norm"])
    v_m = kv[..., MLA_NOPE_DIM:]
    o_m = attend((q_nope, q_pe), (k_nope, kpe_all), v_m, pos, k_pos, chunk_causal)

    y_f = o_f.reshape(B, T, FOX_WIDTH) @ p["w_branch_a"]
    y_m = o_m.reshape(B, T, MLA_WIDTH) @ p["w_branch_b"]
    merged = jax.nn.sigmoid(g_a) * y_f + jax.nn.sigmoid(g_b) * y_m
    return merged @ p["w_out"], (k_f, v_f, logf, c_kv, k_pe)


def block(x, pos, past, p):
    x = x + 0.5 * swiglu_ffn(x, p["ffn1_norm"], p["ffn1_w_gate"], p["ffn1_w_up"], p["ffn1_w_down"])
    mix, new_state = mixer(rmsnorm(x, p["mix_norm"]), pos, past, p)
    x = x + mix
    x = x + 0.5 * swiglu_ffn(x, p["ffn2_norm"], p["ffn2_w_gate"], p["ffn2_w_up"], p["ffn2_w_down"])
    return rmsnorm(x, p["post_norm"]), new_state


def setup_inputs(seed: int = 0) -> dict:
    key = jax.random.key(seed)
    ks = iter(jax.random.split(key, 48))
    L = DEPTH

    def nrm(shape, scale=1.0):
        return scale * jax.random.normal(next(ks), shape, F32)

    def w(shape):
        return nrm(shape, shape[-2] ** -0.5)

    def gain(n):
        return 1.0 + nrm((L, n), 0.05)

    return {
        "x_prompt": nrm((BATCH, SEQ, D_MODEL)),
        "x_sample": nrm((DEC_BATCH, DEC_SEQ, D_MODEL)),
        "cache_fox_k": nrm((L, DEC_BATCH, PAST_LEN, FOX_HEADS, FOX_HEAD_DIM)),
        "cache_fox_v": nrm((L, DEC_BATCH, PAST_LEN, FOX_HEADS, FOX_HEAD_DIM)),
        "cache_fox_logf": jax.nn.log_sigmoid(3.0 + nrm((L, DEC_BATCH, PAST_LEN, FOX_HEADS))),
        "cache_mla_ckv": nrm((L, DEC_BATCH, PAST_LEN, MLA_KV_LORA)),
        "cache_mla_kpe": nrm((L, DEC_BATCH, PAST_LEN, MLA_ROPE_DIM)),
        "ffn1_norm": gain(D_MODEL),
        "ffn1_w_gate": w((L, D_MODEL, D_FF)),
        "ffn1_w_up": w((L, D_MODEL, D_FF)),
        "ffn1_w_down": w((L, D_FF, D_MODEL)),
        "mix_norm": gain(D_MODEL),
        "w_in": w((L, D_MODEL, D_IN)),
        "b_forget": jax.random.uniform(next(ks), (L, FOX_HEADS), F32, 1.0, 5.0),
        "fox_q_norm": gain(FOX_HEAD_DIM),
        "fox_k_norm": gain(FOX_HEAD_DIM),
        "mla_q_lat_norm": gain(MLA_Q_LORA),
        "w_uq": w((L, MLA_Q_LORA, MLA_HEADS * (MLA_NOPE_DIM + MLA_ROPE_DIM))),
        "mla_q_nope_norm": gain(MLA_NOPE_DIM),
        "mla_q_rope_norm": gain(MLA_ROPE_DIM),
        "mla_kv_lat_norm": gain(MLA_KV_LORA),
        "w_ukv": w((L, MLA_KV_LORA, MLA_HEADS * (MLA_NOPE_DIM + MLA_V_DIM))),
        "mla_k_nope_norm": gain(MLA_NOPE_DIM),
        "mla_k_rope_norm": gain(MLA_ROPE_DIM),
        "w_branch_a": w((L, FOX_WIDTH, D_MODEL)),
        "w_branch_b": w((L, MLA_WIDTH, D_MODEL)),
        "w_out": w((L, D_MODEL, D_MODEL)),
        "ffn2_norm": gain(D_MODEL),
        "ffn2_w_gate": w((L, D_MODEL, D_FF)),
        "ffn2_w_up": w((L, D_MODEL, D_FF)),
        "ffn2_w_down": w((L, D_FF, D_MODEL)),
        "post_norm": gain(D_MODEL),
    }


def reference(x_prompt, x_sample, cache_fox_k, cache_fox_v, cache_fox_logf, cache_mla_ckv, cache_mla_kpe,
              ffn1_norm, ffn1_w_gate, ffn1_w_up, ffn1_w_down, mix_norm, w_in, b_forget,
              fox_q_norm, fox_k_norm, mla_q_lat_norm, w_uq, mla_q_nope_norm, mla_q_rope_norm,
              mla_kv_lat_norm, w_ukv, mla_k_nope_norm, mla_k_rope_norm,
              w_branch_a, w_branch_b, w_out, ffn2_norm, ffn2_w_gate, ffn2_w_up, ffn2_w_down, post_norm):
    past_len = cache_fox_k.shape[2]
    pos_p = jnp.arange(x_prompt.shape[1], dtype=jnp.int32)
    pos_s = past_len + jnp.arange(x_sample.shape[1], dtype=jnp.int32)
    y_p, y_s = x_prompt, x_sample
    new_p, new_s = [], []
    for l in range(DEPTH):
        p = dict(
            ffn1_norm=ffn1_norm[l], ffn1_w_gate=ffn1_w_gate[l], ffn1_w_up=ffn1_w_up[l], ffn1_w_down=ffn1_w_down[l],
            mix_norm=mix_norm[l], w_in=w_in[l], b_forget=b_forget[l],
            fox_q_norm=fox_q_norm[l], fox_k_norm=fox_k_norm[l],
            mla_q_lat_norm=mla_q_lat_norm[l], w_uq=w_uq[l],
            mla_q_nope_norm=mla_q_nope_norm[l], mla_q_rope_norm=mla_q_rope_norm[l],
            mla_kv_lat_norm=mla_kv_lat_norm[l], w_ukv=w_ukv[l],
            mla_k_nope_norm=mla_k_nope_norm[l], mla_k_rope_norm=mla_k_rope_norm[l],
            w_branch_a=w_branch_a[l], w_branch_b=w_branch_b[l], w_out=w_out[l],
            ffn2_norm=ffn2_norm[l], ffn2_w_gate=ffn2_w_gate[l], ffn2_w_up=ffn2_w_up[l], ffn2_w_down=ffn2_w_down[l],
            post_norm=post_norm[l])
        y_p, st_p = block(y_p, pos_p, None, p)
        past = (cache_fox_k[l], cache_fox_v[l], cache_fox_logf[l], cache_mla_ckv[l], cache_mla_kpe[l])
        y_s, st_s = block(y_s, pos_s, past, p)
        new_p.append(st_p)
        new_s.append(st_s)

    def stack(states, i):
        return jnp.stack([st[i] for st in states], axis=0)

    return (y_p, y_s,
            stack(new_p, 0), stack(new_p, 1), stack(new_p, 2), stack(new_p, 3), stack(new_p, 4),
            stack(new_s, 0), stack(new_s, 1), stack(new_s, 2), stack(new_s, 3), stack(new_s, 4))
```

```python
import functools
import math

import jax
import jax.numpy as jnp
from jax import lax
from jax.experimental import pallas as pl
from jax.experimental.pallas import tpu as pltpu

F32 = jnp.float32
BF16 = jnp.bfloat16

NORM_EPS = 1e-6
CHUNK = 64
ROPE_THETA = 10000.0
MASKED = -1e30
LOG2E = math.log2(math.e)

LANES = 128
MXU_DIM = 256
VMEM_LIMIT_BYTES = 56 * 1024 * 1024
CUMSUM_BLOCK = MXU_DIM


def _params(*semantics):
    return pltpu.CompilerParams(dimension_semantics=semantics, vmem_limit_bytes=VMEM_LIMIT_BYTES)


def _tile(n, pref):
    t = min(n, pref)
    while n % t:
        t -= 8
    return t


def _rms(x, g):
    return x * lax.rsqrt(jnp.mean(x * x, axis=-1, keepdims=True) + NORM_EPS) * g


def _dot(a, b):
    return jnp.dot(a, b, preferred_element_type=F32)


def _dot_nt(a, b):
    return lax.dot_general(a, b, (((1,), (1,)), ((), ())), preferred_element_type=F32)


def _split3(x):
    hi = x.astype(BF16)
    r = x - hi.astype(F32)
    mid = r.astype(BF16)
    lo = (r - mid.astype(F32)).astype(BF16)
    return hi, mid, lo


def _ffn_kernel(x_ref, g_ref, wg_ref, wu_ref, wd_ref, g2_ref, o_ref, *rest, post):
    if post:
        (h_scr,) = rest
    else:
        h2_ref, h_scr = rest
    j = pl.program_id(1)

    @pl.when(j == 0)
    def _():
        h_scr[...] = _rms(x_ref[...], g_ref[...]).astype(BF16)
        o_ref[...] = jnp.zeros_like(o_ref)

    h = h_scr[...]
    g = _dot(h, wg_ref[...])
    u = _dot(h, wu_ref[...])
    a = (g * jax.nn.sigmoid(g) * u).astype(BF16)
    o_ref[...] += _dot(a, wd_ref[...])

    @pl.when(j == pl.num_programs(1) - 1)
    def _():
        y = x_ref[...] + 0.5 * o_ref[...]
        if post:
            o_ref[...] = _rms(y, g2_ref[...])
        else:
            o_ref[...] = y
            h2_ref[...] = _rms(y, g2_ref[...]).astype(BF16)


def _ffn(x, g, wg, wu, wd, g2, *, post, tm, tf):
    n, d = x.shape
    dff = wg.shape[1]
    grid = (n // tm, dff // tf)
    row = pl.BlockSpec((tm, d), lambda i, j: (i, 0))
    vec = pl.BlockSpec((1, d), lambda i, j: (0, 0))
    out_shape = [jax.ShapeDtypeStruct((n, d), F32)]
    out_specs = [row]
    if not post:
        out_shape.append(jax.ShapeDtypeStruct((n, d), BF16))
        out_specs.append(row)
    res = pl.pallas_call(
        functools.partial(_ffn_kernel, post=post),
        grid=grid,
        in_specs=[row, vec,
                  pl.BlockSpec((d, tf), lambda i, j: (0, j)),
                  pl.BlockSpec((d, tf), lambda i, j: (0, j)),
                  pl.BlockSpec((tf, d), lambda i, j: (j, 0)),
                  vec],
        out_specs=out_specs,
        out_shape=out_shape,
        scratch_shapes=[pltpu.VMEM((tm, d), BF16)],
        compiler_params=_params("parallel", "arbitrary"),
        name="ffn_post" if post else "ffn",
    )(x, g, wg, wu, wd, g2)
    return res[0] if post else res


def _bias_lanes(fs, head, *, query):
    col = fs[:, head:head + 1]
    hi, mid, lo = (p.astype(F32) for p in _split3(col))
    lane = lax.broadcasted_iota(jnp.int32, (fs.shape[0], LANES), 1)
    if not query:
        hi, mid, lo = -hi, -mid, -lo
        lane = lane - 3
    vals = jnp.where(lane == 0, hi, jnp.where(lane == 1, mid, jnp.where(lane == 2, lo, 0.0)))
    ones = (lane >= 3) & (lane < 6) if query else (lane >= -3) & (lane < 0)
    return jnp.where(ones, 1.0, vals)


def _fox_proj_kernel(*refs, mode, heads, hd, with_bias, inv_scale, tk):
    it = iter(refs)
    h_ref, w_ref = next(it), next(it)
    g_ref = next(it) if mode != "v" else None
    f_ref = next(it) if with_bias else None
    state_in = next(it) if mode != "q" else None
    state_ref = next(it) if mode != "q" else None
    o_ref = next(it, None)
    del state_in
    z = _dot(h_ref[...], w_ref[...])
    if with_bias:
        fs = f_ref[...] * inv_scale
    outs = []
    for hh in range(heads):
        zh = z[:, hh * hd:(hh + 1) * hd]
        if mode != "v":
            zh = _rms(zh, g_ref[...])
        outs.append(zh)
        if o_ref is None:
            continue
        if mode == "v" and tk:
            for c in range(zh.shape[0] // tk):
                o_ref[hh, c] = zh[c * tk:(c + 1) * tk, :].T.astype(BF16)
        elif with_bias:
            o_ref[hh] = jnp.concatenate([zh, _bias_lanes(fs, hh, query=(mode == "q"))], axis=1).astype(BF16)
    if mode != "q":
        state_ref[...] = jnp.concatenate(outs, axis=1)
    elif not with_bias:
        o_ref[...] = jnp.concatenate(outs, axis=1).astype(BF16)


def _fox_proj(h, w, g, fcum, state, layer, *, mode, heads, hd, prompt, tm, tk):
    n, d = h.shape
    width = heads * hd
    with_bias = prompt and mode != "v"
    row = lambda c: pl.BlockSpec((tm, c), lambda i: (i, 0))
    ins, in_specs = [h, w], [row(d), pl.BlockSpec((d, width), lambda i: (0, 0))]
    if mode != "v":
        ins.append(g)
        in_specs.append(pl.BlockSpec((1, hd), lambda i: (0, 0)))
    if with_bias:
        ins.append(fcum)
        in_specs.append(row(heads))
    out_shape, out_specs, aliases = [], [], {}
    if mode != "q":
        aliases = {len(ins): 0}
        ins.append(state)
        in_specs.append(pl.BlockSpec(memory_space=pl.ANY))
        out_shape.append(jax.ShapeDtypeStruct(state.shape, F32))
        out_specs.append(pl.BlockSpec((None, tm, width), lambda i: (layer, i, 0)))
    if prompt and mode == "v":
        out_shape.append(jax.ShapeDtypeStruct((heads, n // tk, hd, tk), BF16))
        out_specs.append(pl.BlockSpec((heads, tm // tk, hd, tk), lambda i: (0, i, 0, 0)))
    elif with_bias:
        out_shape.append(jax.ShapeDtypeStruct((heads, n, hd + LANES), BF16))
        out_specs.append(pl.BlockSpec((heads, tm, hd + LANES), lambda i: (0, i, 0)))
    elif mode == "q":
        out_shape.append(jax.ShapeDtypeStruct((n, width), BF16))
        out_specs.append(row(width))
    res = pl.pallas_call(
        functools.partial(_fox_proj_kernel, mode=mode, heads=heads, hd=hd, with_bias=with_bias,
                          inv_scale=float(hd) ** 0.5, tk=tk if prompt else 0),
        grid=(n // tm,),
        in_specs=in_specs,
        out_specs=out_specs,
        out_shape=out_shape,
        input_output_aliases=aliases,
        compiler_params=_params("parallel"),
        name="fox_" + mode,
    )(*ins)
    return res


def _rotary(y, cos, sin):
    return y * cos + pltpu.roll(y, LANES // 2, 1) * sin


def _mla_q_kernel(h_ref, wcq_ref, gq_ref, wuq_ref, gh_ref, cos_ref, sin_ref, o_ref, cq_scr, *, hg, nope):
    @pl.when(pl.program_id(1) == 0)
    def _():
        cq_scr[...] = _rms(_dot(h_ref[...], wcq_ref[...]), gq_ref[...]).astype(BF16)

    z = _dot(cq_scr[...], wuq_ref[...])
    gh = gh_ref[...]
    per = nope + LANES
    for hh in range(hg):
        qn = _rms(z[:, hh * per:hh * per + nope], gh[:, :nope])
        qr = _rotary(_rms(z[:, hh * per + nope:(hh + 1) * per], gh[:, nope:]), cos_ref[...], sin_ref[...])
        o_ref[hh] = jnp.concatenate([qn, qr], axis=1).astype(BF16)


def _mla_q(h, wcq, gq, wuq, gh, cos, sin, *, heads, nope, tm, hg):
    n, d = h.shape
    ql = wcq.shape[1]
    per = nope + LANES
    return pl.pallas_call(
        functools.partial(_mla_q_kernel, hg=hg, nope=nope),
        grid=(n // tm, heads // hg),
        in_specs=[pl.BlockSpec((tm, d), lambda i, j: (i, 0)),
                  pl.BlockSpec((d, ql), lambda i, j: (0, 0)),
                  pl.BlockSpec((1, ql), lambda i, j: (0, 0)),
                  pl.BlockSpec((ql, hg * per), lambda i, j: (0, j)),
                  pl.BlockSpec((1, per), lambda i, j: (0, 0)),
                  pl.BlockSpec((tm, LANES), lambda i, j: (i, 0)),
                  pl.BlockSpec((tm, LANES), lambda i, j: (i, 0))],
        out_specs=pl.BlockSpec((hg, tm, per), lambda i, j: (j, i, 0)),
        out_shape=jax.ShapeDtypeStruct((heads, n, per), BF16),
        scratch_shapes=[pltpu.VMEM((tm, ql), BF16)],
        compiler_params=_params("parallel", "arbitrary"),
        name="mla_q",
    )(h, wcq, gq, wuq, gh, cos, sin)


def _log_sigmoid(x):
    return jnp.minimum(x, 0.0) - jnp.log1p(jnp.exp(-jnp.abs(x)))


def _mla_kv_kernel(*refs, up, hg, kvl, rope, fh, nope, tk):
    it = iter(refs)
    h_ref, w6_ref, gkv_ref, gk_ref, b_ref, cos_ref, sin_ref = (next(it) for _ in range(7))
    if up:
        wukv_ref, gkn_ref = next(it), next(it)
    for _ in range(3):
        next(it)
    ckv_o, kpe_o, logf_o = next(it), next(it), next(it)
    if up:
        kcat_o, vt_o, ckv_scr, kpe_scr = (next(it) for _ in range(4))

    @pl.when(pl.program_id(1) == 0)
    def _():
        z = _dot(h_ref[...], w6_ref[...])
        ckv = _rms(z[:, :kvl], gkv_ref[...])
        ckv_o[...] = ckv
        rot = _rotary(_rms(z[:, kvl:kvl + LANES], gk_ref[...]), cos_ref[...], sin_ref[...])
        kpe_o[...] = rot[:, :rope]
        logf_o[...] = _log_sigmoid(z[:, kvl + LANES:] + b_ref[...])[:, :fh]
        if up:
            ckv_scr[...] = ckv.astype(BF16)
            kpe_scr[...] = rot.astype(BF16)

    if up:
        kv = _dot(ckv_scr[...], wukv_ref[...])
        per = kv.shape[1] // hg
        for hh in range(hg):
            kn = _rms(kv[:, hh * per:hh * per + nope], gkn_ref[...]).astype(BF16)
            kcat_o[hh] = jnp.concatenate([kn, kpe_scr[...]], axis=1)
            v = kv[:, hh * per + nope:(hh + 1) * per]
            for c in range(v.shape[0] // tk):
                vt_o[hh, c] = v[c * tk:(c + 1) * tk, :].T.astype(BF16)


def _mla_kv(h, w6, gkv, gk, b, cos, sin, wukv, gkn, states, layer, *, up, heads, nope, vdim, rope, fh, tm, tk, hg):
    n, d = h.shape
    kvl = gkv.shape[1]
    nj = heads // hg if up else 1
    c2 = lambda shape: pl.BlockSpec(shape, lambda i, j: (0, 0))
    ins = [h, w6, gkv, gk, b, cos, sin]
    in_specs = [pl.BlockSpec((tm, d), lambda i, j: (i, 0)), c2(w6.shape), c2((1, kvl)), c2((1, LANES)),
                c2((1, LANES)), pl.BlockSpec((tm, LANES), lambda i, j: (i, 0)),
                pl.BlockSpec((tm, LANES), lambda i, j: (i, 0))]
    if up:
        per = nope + vdim
        ins += [wukv, gkn]
        in_specs += [pl.BlockSpec((kvl, hg * per), lambda i, j: (0, j)), c2((1, nope))]
    aliases = {len(ins) + k: k for k in range(3)}
    ins += list(states)
    in_specs += [pl.BlockSpec(memory_space=pl.ANY)] * 3
    out_shape = [jax.ShapeDtypeStruct(s.shape, F32) for s in states]
    out_specs = [pl.BlockSpec((None, tm, w), lambda i, j: (layer, i, 0)) for w in (kvl, rope, fh)]
    scratch = []
    if up:
        out_shape += [jax.ShapeDtypeStruct((heads, n, nope + LANES), BF16),
                      jax.ShapeDtypeStruct((heads, n // tk, vdim, tk), BF16)]
        out_specs += [pl.BlockSpec((hg, tm, nope + LANES), lambda i, j: (j, i, 0)),
                      pl.BlockSpec((hg, tm // tk, vdim, tk), lambda i, j: (j, i, 0, 0))]
        scratch = [pltpu.VMEM((tm, kvl), BF16), pltpu.VMEM((tm, LANES), BF16)]
    return pl.pallas_call(
        functools.partial(_mla_kv_kernel, up=up, hg=hg, kvl=kvl, rope=rope, fh=fh, nope=nope, tk=tk),
        grid=(n // tm, nj),
        in_specs=in_specs,
        out_specs=out_specs,
        out_shape=out_shape,
        input_output_aliases=aliases,
        scratch_shapes=scratch,
        compiler_params=_params("parallel", "arbitrary"),
        name="mla_kv",
    )(*ins)


def _cumsum_kernel(x_ref, tri_ref, o_ref):
    tri = tri_ref[...]
    carry = jnp.zeros((x_ref.shape[0], 1), F32)
    for c in range(x_ref.shape[1] // CUMSUM_BLOCK):
        sl = slice(c * CUMSUM_BLOCK, (c + 1) * CUMSUM_BLOCK)
        hi, mid, lo = _split3(x_ref[:, sl])
        blk = _dot(hi, tri) + _dot(mid, tri) + _dot(lo, tri) + carry
        o_ref[:, sl] = blk
        carry = blk[:, CUMSUM_BLOCK - 1:]


def _cumsum_time(x):
    b, hh, s = x.shape
    r = lax.broadcasted_iota(jnp.int32, (CUMSUM_BLOCK, CUMSUM_BLOCK), 0)
    c = lax.broadcasted_iota(jnp.int32, (CUMSUM_BLOCK, CUMSUM_BLOCK), 1)
    tri = (r <= c).astype(BF16)
    return pl.pallas_call(
        _cumsum_kernel,
        grid=(b,),
        in_specs=[pl.BlockSpec((None, hh, s), lambda i: (i, 0, 0)),
                  pl.BlockSpec((CUMSUM_BLOCK, CUMSUM_BLOCK), lambda i: (0, 0))],
        out_specs=pl.BlockSpec((None, hh, s), lambda i: (i, 0, 0)),
        out_shape=jax.ShapeDtypeStruct(x.shape, F32),
        compiler_params=_params("parallel"),
        name="cumsum_logf",
    )(x, tri)


def _attn_prompt_kernel(q_ref, k_ref, vt_ref, o_ref, m_scr, l_scr, acc_scr, *, tq, c, chunk):
    i = pl.program_id(2)
    q = q_ref[...]
    m_scr[...] = jnp.full_like(m_scr, MASKED)
    l_scr[...] = jnp.zeros_like(l_scr)
    acc_scr[...] = jnp.zeros_like(acc_scr)

    def step(j, diagonal):
        k = k_ref[pl.ds(pl.multiple_of(j * tq, tq), tq), :]
        t = _dot_nt(k, q) * c
        if diagonal:
            ks = lax.broadcasted_iota(jnp.int32, t.shape, 0)
            qs = lax.broadcasted_iota(jnp.int32, t.shape, 1)
            vis = (ks // chunk) <= (qs // chunk) if chunk > 1 else ks <= qs
            t = jnp.where(vis, t, MASKED)
        m_old = m_scr[...]
        m_new = jnp.maximum(m_old, jnp.max(t, axis=0, keepdims=True))
        alpha = jnp.exp2(m_old - m_new)
        p = jnp.exp2(t - m_new)
        l_scr[...] = alpha * l_scr[...] + jnp.sum(p, axis=0, keepdims=True)
        acc_scr[...] = alpha * acc_scr[...] + _dot(vt_ref[j], p.astype(BF16))
        m_scr[...] = m_new

    def body(j, carry):
        step(j, False)
        return carry

    lax.fori_loop(0, i, body, 0)
    step(i, True)
    o_ref[...] = (acc_scr[...] / l_scr[...]).T.astype(o_ref.dtype)


def _attn_prompt(q, k, vt, *, batch, seq, tq, scale, chunk):
    heads, n, dk = q.shape
    dv = vt.shape[2]
    nq = seq // tq
    return pl.pallas_call(
        functools.partial(_attn_prompt_kernel, tq=tq, c=scale * LOG2E, chunk=chunk),
        grid=(batch, heads, nq),
        in_specs=[pl.BlockSpec((None, tq, dk), lambda b, h, i: (h, b * nq + i, 0)),
                  pl.BlockSpec((None, seq, dk), lambda b, h, i: (h, b, 0)),
                  pl.BlockSpec((None, nq, dv, tq), lambda b, h, i: (h, b, 0, 0))],
        out_specs=pl.BlockSpec((tq, dv), lambda b, h, i: (b * nq + i, h)),
        out_shape=jax.ShapeDtypeStruct((n, heads * dv), BF16),
        scratch_shapes=[pltpu.VMEM((1, tq), F32), pltpu.VMEM((1, tq), F32), pltpu.VMEM((dv, tq), F32)],
        compiler_params=_params("parallel", "parallel", "arbitrary"),
        name="attn_prompt_chunk%d" % chunk,
    )(q, k, vt)


def _softmax_two_segments(s_c, s_n, v_c, v_n):
    m = jnp.maximum(jnp.max(s_c, axis=1, keepdims=True), jnp.max(s_n, axis=1, keepdims=True))
    p_c = jnp.exp(s_c - m)
    p_n = jnp.exp(s_n - m)
    l = jnp.sum(p_c, axis=1, keepdims=True) + jnp.sum(p_n, axis=1, keepdims=True)
    return (_dot(p_c.astype(BF16), v_c) + _dot(p_n.astype(BF16), v_n)) / l


def _fox_sample_kernel(q_ref, kc_ref, vc_ref, kn_ref, vn_ref, fq_ref, fkc_ref, fkn_ref, o_ref, *, hg, hd, scale):
    t = q_ref.shape[0]
    rows = lax.broadcasted_iota(jnp.int32, (t, t), 0)
    cols = lax.broadcasted_iota(jnp.int32, (t, t), 1)
    outs = []
    for hh in range(hg):
        sl = slice(hh * hd, (hh + 1) * hd)
        q = q_ref[:, sl]
        fq = fq_ref[hh]
        s_c = _dot_nt(q, kc_ref[:, sl].astype(BF16)) * scale + (fq - fkc_ref[hh])
        s_n = _dot_nt(q, kn_ref[:, sl].astype(BF16)) * scale + (fq - fkn_ref[hh])
        s_n = jnp.where(cols <= rows, s_n, MASKED)
        outs.append(_softmax_two_segments(s_c, s_n, vc_ref[:, sl].astype(BF16), vn_ref[:, sl].astype(BF16)))
    o_ref[...] = jnp.concatenate(outs, axis=1).astype(o_ref.dtype)


def _fox_sample(q, kc, vc, k_state, v_state, layer, fq, fkc, fkn, *, heads, hd, hg):
    n, width = q.shape
    b, s, _ = kc.shape
    t = n // b
    gw = hg * hd
    new = pl.BlockSpec((None, t, gw), lambda i, g: (layer, i, g))
    return pl.pallas_call(
        functools.partial(_fox_sample_kernel, hg=hg, hd=hd, scale=float(hd) ** -0.5),
        grid=(b, heads // hg),
        in_specs=[pl.BlockSpec((t, gw), lambda i, g: (i, g)),
                  pl.BlockSpec((None, s, gw), lambda i, g: (i, 0, g)),
                  pl.BlockSpec((None, s, gw), lambda i, g: (i, 0, g)),
                  new, new,
                  pl.BlockSpec((None, hg, t, 1), lambda i, g: (i, g, 0, 0)),
                  pl.BlockSpec((None, hg, 1, s), lambda i, g: (i, g, 0, 0)),
                  pl.BlockSpec((None, hg, 1, t), lambda i, g: (i, g, 0, 0))],
        out_specs=pl.BlockSpec((t, gw), lambda i, g: (i, g)),
        out_shape=jax.ShapeDtypeStruct((n, width), BF16),
        compiler_params=_params("parallel", "parallel"),
        name="fox_sample",
    )(q, kc, vc, k_state, v_state, fq, fkc, fkn)


def _mla_sample_kernel(q_ref, cc_ref, pc_ref, cn_ref, pn_ref, wukv_ref, gkn_ref, o_ref, *, hg, nope, past, scale, chunk):
    t = q_ref.shape[1]
    rows = lax.broadcasted_iota(jnp.int32, (t, t), 0) + past
    cols = lax.broadcasted_iota(jnp.int32, (t, t), 1) + past
    vis = (cols // chunk) <= (rows // chunk)
    per = wukv_ref.shape[1] // hg

    def keys_values(c_ref, p_ref):
        kv = _dot(c_ref[...].astype(BF16), wukv_ref[...])
        pe = p_ref[...].astype(BF16)
        pad = jnp.zeros((pe.shape[0], LANES - pe.shape[1]), BF16)
        return kv, jnp.concatenate([pe, pad], axis=1)

    kv_c, pe_c = keys_values(cc_ref, pc_ref)
    kv_n, pe_n = keys_values(cn_ref, pn_ref)
    outs = []
    for hh in range(hg):
        q = q_ref[hh]

        def keys(kv, pe):
            kn = _rms(kv[:, hh * per:hh * per + nope], gkn_ref[...]).astype(BF16)
            return jnp.concatenate([kn, pe], axis=1)

        s_c = _dot_nt(q, keys(kv_c, pe_c)) * scale
        s_n = jnp.where(vis, _dot_nt(q, keys(kv_n, pe_n)) * scale, MASKED)
        v_c = kv_c[:, hh * per + nope:(hh + 1) * per].astype(BF16)
        v_n = kv_n[:, hh * per + nope:(hh + 1) * per].astype(BF16)
        outs.append(_softmax_two_segments(s_c, s_n, v_c, v_n))
    o_ref[...] = jnp.concatenate(outs, axis=1).astype(o_ref.dtype)


def _mla_sample(q, ckv_cache, kpe_cache, ckv_state, kpe_state, layer, wukv, gkn, *, heads, nope, vdim, hg, chunk):
    _, n, dk = q.shape
    b, s, kvl = ckv_cache.shape
    rope = kpe_cache.shape[2]
    t = n // b
    per = nope + vdim
    return pl.pallas_call(
        functools.partial(_mla_sample_kernel, hg=hg, nope=nope, past=s, scale=float(nope + rope) ** -0.5, chunk=chunk),
        grid=(b, heads // hg),
        in_specs=[pl.BlockSpec((hg, t, dk), lambda i, g: (g, i, 0)),
                  pl.BlockSpec((None, s, kvl), lambda i, g: (i, 0, 0)),
                  pl.BlockSpec((None, s, rope), lambda i, g: (i, 0, 0)),
                  pl.BlockSpec((None, t, kvl), lambda i, g: (layer, i, 0)),
                  pl.BlockSpec((None, t, rope), lambda i, g: (layer, i, 0)),
                  pl.BlockSpec((kvl, hg * per), lambda i, g: (0, g)),
                  pl.BlockSpec((1, nope), lambda i, g: (0, 0))],
        out_specs=pl.BlockSpec((t, hg * vdim), lambda i, g: (i, g)),
        out_shape=jax.ShapeDtypeStruct((n, heads * vdim), BF16),
        compiler_params=_params("parallel", "arbitrary"),
        name="mla_sample",
    )(q, ckv_cache, kpe_cache, ckv_state, kpe_state, wukv, gkn)


def _merge_kernel(x_ref, h_ref, of_ref, om_ref, wga_ref, wgb_ref, wa_ref, wb_ref, wo_ref, o_ref):
    j = pl.program_id(1)

    @pl.when(j == 0)
    def _():
        o_ref[...] = jnp.zeros_like(o_ref)

    h = h_ref[...]
    ga = jax.nn.sigmoid(_dot(h, wga_ref[...]))
    gb = jax.nn.sigmoid(_dot(h, wgb_ref[...]))
    m = ga * _dot(of_ref[...], wa_ref[...]) + gb * _dot(om_ref[...], wb_ref[...])
    o_ref[...] += _dot(m.astype(BF16), wo_ref[...])

    @pl.when(j == pl.num_programs(1) - 1)
    def _():
        o_ref[...] = x_ref[...] + o_ref[...]


def _merge(x, h, of, om, wga, wgb, wa, wb, wo, *, tm, tn):
    n, d = x.shape
    row = lambda c: pl.BlockSpec((tm, c), lambda i, j: (i, 0))
    col = lambda r: pl.BlockSpec((r, tn), lambda i, j: (0, j))
    return pl.pallas_call(
        _merge_kernel,
        grid=(n // tm, d // tn),
        in_specs=[row(d), row(d), row(of.shape[1]), row(om.shape[1]),
                  col(d), col(d), col(of.shape[1]), col(om.shape[1]),
                  pl.BlockSpec((tn, d), lambda i, j: (j, 0))],
        out_specs=row(d),
        out_shape=jax.ShapeDtypeStruct((n, d), F32),
        compiler_params=_params("parallel", "arbitrary"),
        name="merge_out",
    )(x, h, of, om, wga, wgb, wa, wb, wo)


def _swap_halves(a):
    half = a.shape[-1] // 2
    return jnp.concatenate([a[..., half:], a[..., :half]], axis=-1)


def _rope_tables(pos, rope):
    half = rope // 2
    inv = ROPE_THETA ** (-jnp.arange(half, dtype=F32) / half)
    ang = pos.astype(F32)[:, None] * inv[None, :]
    cos, sin = jnp.cos(ang), jnp.sin(ang)
    pad = jnp.zeros((pos.shape[0], LANES - rope), F32)
    return jnp.concatenate([cos, cos, pad], axis=1), jnp.concatenate([-sin, sin, pad], axis=1)


def kernel(x_prompt, x_sample, cache_fox_k, cache_fox_v, cache_fox_logf, cache_mla_ckv, cache_mla_kpe, ffn1_norm, ffn1_w_gate, ffn1_w_up, ffn1_w_down, mix_norm, w_in, b_forget, fox_q_norm, fox_k_norm, mla_q_lat_norm, w_uq, mla_q_nope_norm, mla_q_rope_norm, mla_kv_lat_norm, w_ukv, mla_k_nope_norm, mla_k_rope_norm, w_branch_a, w_branch_b, w_out, ffn2_norm, ffn2_w_gate, ffn2_w_up, ffn2_w_down, post_norm):
    depth, d = ffn1_norm.shape
    bp, tp, _ = x_prompt.shape
    bs, ts, _ = x_sample.shape
    past, fh, hd = cache_fox_k.shape[2], cache_fox_k.shape[3], cache_fox_k.shape[4]
    fw = fh * hd
    ql, kvl = mla_q_lat_norm.shape[1], mla_kv_lat_norm.shape[1]
    nope, rope = mla_q_nope_norm.shape[1], mla_q_rope_norm.shape[1]
    mh = w_uq.shape[2] // (nope + rope)
    vdim = w_ukv.shape[2] // mh - nope
    assert nope == LANES and 2 * rope == LANES and hd == LANES and vdim == LANES
    np_, ns = bp * tp, bs * ts

    bf = lambda a: a.astype(BF16)
    o = [0]
    for wdt in (fw, fw, fw, fh, ql, kvl, rope, d, d):
        o.append(o[-1] + wdt)
    w_q, w_k, w_v = (bf(w_in[:, :, o[k]:o[k + 1]]) for k in range(3))
    w_f, w_cq, w_ckv, w_kpe = (w_in[:, :, o[k]:o[k + 1]] for k in range(3, 7))
    w_ga, w_gb = bf(w_in[:, :, o[7]:o[8]]), bf(w_in[:, :, o[8]:o[9]])
    w_cq = bf(w_cq)
    w6 = bf(jnp.concatenate([w_ckv, w_kpe, _swap_halves(w_kpe), w_f,
                             jnp.zeros((depth, d, LANES - fh), F32)], axis=2))
    b6 = jnp.concatenate([b_forget, jnp.zeros((depth, LANES - fh), F32)], axis=1)
    uq = w_uq.reshape(depth, ql, mh, nope + rope)
    wuq = bf(jnp.concatenate([uq, _swap_halves(uq[..., nope:])], axis=3).reshape(depth, ql, mh * (nope + LANES)))
    g_qh = jnp.concatenate([mla_q_nope_norm, mla_q_rope_norm, _swap_halves(mla_q_rope_norm)], axis=1)
    g_kr = jnp.concatenate([mla_k_rope_norm, _swap_halves(mla_k_rope_norm)], axis=1)
    wukv = bf(w_ukv)
    f1g, f1u, f1d = bf(ffn1_w_gate), bf(ffn1_w_up), bf(ffn1_w_down)
    f2g, f2u, f2d = bf(ffn2_w_gate), bf(ffn2_w_up), bf(ffn2_w_down)
    wa, wb, wo = bf(w_branch_a), bf(w_branch_b), bf(w_out)

    cos_p, sin_p = (jnp.tile(a, (bp, 1)) for a in _rope_tables(jnp.arange(tp, dtype=jnp.int32), rope))
    cos_s, sin_s = (jnp.tile(a, (bs, 1)) for a in _rope_tables(past + jnp.arange(ts, dtype=jnp.int32), rope))

    def states(n):
        return [jnp.zeros((depth, n, w), F32) for w in (fw, fw, fh, kvl, rope)]

    st_p, st_s = states(np_), states(ns)

    tq = _tile(tp, 512)
    tm_p, tm_s = _tile(np_, 512), _tile(ns, 512)
    assert tm_p % tq == 0 and tp % tq == 0
    tf = _tile(f1g.shape[2], 512)
    tn = _tile(d, 256)
    s_pad = -(-(past + ts) // CUMSUM_BLOCK) * CUMSUM_BLOCK

    def vec(a, l):
        return a[l][None, :]

    def tokens(x, st, l, *, prompt):
        tm = tm_p if prompt else tm_s
        cos, sin = (cos_p, sin_p) if prompt else (cos_s, sin_s)
        n = x.shape[0]
        x1, h = _ffn(x, vec(ffn1_norm, l), f1g[l], f1u[l], f1d[l], vec(mix_norm, l), post=False, tm=tm, tf=tf)
        res = _mla_kv(h, w6[l], vec(mla_kv_lat_norm, l), vec(g_kr, l), vec(b6, l), cos, sin, wukv[l],
                      vec(mla_k_nope_norm, l), (st[3], st[4], st[2]), l, up=prompt, heads=mh, nope=nope,
                      vdim=vdim, rope=rope, fh=fh, tm=tm, tk=tq, hg=4)
        st[3], st[4], st[2] = res[:3]
        q_m = _mla_q(h, w_cq[l], vec(mla_q_lat_norm, l), wuq[l], vec(g_qh, l), cos, sin,
                     heads=mh, nope=nope, tm=tm, hg=4)
        proj = functools.partial(_fox_proj, heads=fh, hd=hd, prompt=prompt, tm=tm, tk=tq)
        if prompt:
            logf_t = jnp.swapaxes(st[2][l].reshape(bp, tp, fh), 1, 2)
            fcum = jnp.swapaxes(_cumsum_time(logf_t), 1, 2).reshape(n, fh)
            (q_f,) = proj(h, w_q[l], vec(fox_q_norm, l), fcum, None, l, mode="q")
            st[0], k_f = proj(h, w_k[l], vec(fox_k_norm, l), fcum, st[0], l, mode="k")
            st[1], vt_f = proj(h, w_v[l], None, None, st[1], l, mode="v")
            o_f = _attn_prompt(q_f, k_f, vt_f, batch=bp, seq=tp, tq=tq, scale=float(hd) ** -0.5, chunk=1)
            o_m = _attn_prompt(q_m, res[3], res[4], batch=bp, seq=tp, tq=tq,
                               scale=float(nope + rope) ** -0.5, chunk=CHUNK)
        else:
            (q_f,) = proj(h, w_q[l], vec(fox_q_norm, l), None, None, l, mode="q")
            (st[0],) = proj(h, w_k[l], vec(fox_k_norm, l), None, st[0], l, mode="k")
            (st[1],) = proj(h, w_v[l], None, None, st[1], l, mode="v")
            logf_all = jnp.concatenate([cache_fox_logf[l], st[2][l].reshape(bs, ts, fh),
                                        jnp.zeros((bs, s_pad - past - ts, fh), F32)], axis=1)
            fall = _cumsum_time(jnp.swapaxes(logf_all, 1, 2))
            fkc = fall[:, :, None, :past]
            fkn = fall[:, :, None, past:past + ts]
            fq = fall[:, :, past:past + ts, None]
            o_f = _fox_sample(q_f, cache_fox_k[l].reshape(bs, past, fw), cache_fox_v[l].reshape(bs, past, fw),
                              st[0], st[1], l, fq, fkc, fkn, heads=fh, hd=hd, hg=4)
            o_m = _mla_sample(q_m, cache_mla_ckv[l], cache_mla_kpe[l], st[3], st[4], l, wukv[l],
                              vec(mla_k_nope_norm, l), heads=mh, nope=nope, vdim=vdim, hg=4, chunk=CHUNK)
        x2 = _merge(x1, h, o_f, o_m, w_ga[l], w_gb[l], wa[l], wb[l], wo[l], tm=tm, tn=tn)
        return _ffn(x2, vec(ffn2_norm, l), f2g[l], f2u[l], f2d[l], vec(post_norm, l), post=True, tm=tm, tf=tf)

    y_p, y_s = x_prompt.reshape(np_, d), x_sample.reshape(ns, d)
    for l in range(depth):
        y_p = tokens(y_p, st_p, l, prompt=True)
        y_s = tokens(y_s, st_s, l, prompt=False)

    def shaped(st, b, t):
        return (st[0].reshape(depth, b, t, fh, hd), st[1].reshape(depth, b, t, fh, hd), st[2].reshape(depth, b, t, fh),
                st[3].reshape(depth, b, t, kvl), st[4].reshape(depth, b, t, rope))

    return (y_p.reshape(bp, tp, d), y_s.reshape(bs, ts, d)) + shaped(st_p, bp, tp) + shaped(st_s, bs, ts)
```

```python
import functools
import math

import jax
import jax.numpy as jnp
from jax import lax
from jax.experimental import pallas as pl
from jax.experimental.pallas import tpu as pltpu

F32 = jnp.float32
BF16 = jnp.bfloat16

NORM_EPS = 1e-6
CHUNK = 64
ROPE_THETA = 10000.0
MASKED = -1e30
LOG2E = math.log2(math.e)

LANES = 128
MXU_DIM = 256
VMEM_LIMIT_BYTES = 56 * 1024 * 1024
CUMSUM_BLOCK = MXU_DIM
ATTN_TILE = 512
ATTN_HEADS = 2


def _params(*semantics):
    return pltpu.CompilerParams(dimension_semantics=semantics, vmem_limit_bytes=VMEM_LIMIT_BYTES)


def _tile(n, pref):
    t = min(n, pref)
    while n % t:
        t -= 8
    return t


def _rms(x, g, axis=-1):
    return x * lax.rsqrt(jnp.mean(x * x, axis=axis, keepdims=True) + NORM_EPS) * g


def _dot(a, b):
    return jnp.dot(a, b, preferred_element_type=F32)


def _dot_nt(a, b):
    return lax.dot_general(a, b, (((1,), (1,)), ((), ())), preferred_element_type=F32)


def _split3(x):
    hi = x.astype(BF16)
    r = x - hi.astype(F32)
    mid = r.astype(BF16)
    lo = (r - mid.astype(F32)).astype(BF16)
    return hi, mid, lo


def _layer_spec(shape, layer, index_map):
    return pl.BlockSpec((None,) + tuple(shape), lambda *ids: (layer,) + tuple(index_map(*ids)))


def _ffn_kernel(x_ref, g_ref, wg_ref, wu_ref, wd_ref, g2_ref, o_ref, *rest, post):
    if post:
        (h_scr,) = rest
    else:
        h2_ref, h_scr = rest
    j = pl.program_id(1)

    @pl.when(j == 0)
    def _():
        h_scr[...] = _rms(x_ref[...], g_ref[...]).astype(BF16)
        o_ref[...] = jnp.zeros_like(o_ref)

    h = h_scr[...]
    g = _dot(h, wg_ref[...])
    u = _dot(h, wu_ref[...])
    a = (g * jax.nn.sigmoid(g) * u).astype(BF16)
    o_ref[...] += _dot(a, wd_ref[...])

    @pl.when(j == pl.num_programs(1) - 1)
    def _():
        y = x_ref[...] + 0.5 * o_ref[...]
        if post:
            o_ref[...] = _rms(y, g2_ref[...])
        else:
            o_ref[...] = y
            h2_ref[...] = _rms(y, g2_ref[...]).astype(BF16)


def _ffn(x, g, wg, wu, wd, g2, layer, *, post, tm, tf):
    n, d = x.shape
    dff = wg.shape[2]
    row = pl.BlockSpec((tm, d), lambda i, j: (i, 0))
    vec = _layer_spec((1, d), layer, lambda i, j: (0, 0))
    out_shape = [jax.ShapeDtypeStruct((n, d), F32)]
    out_specs = [row]
    if not post:
        out_shape.append(jax.ShapeDtypeStruct((n, d), BF16))
        out_specs.append(row)
    res = pl.pallas_call(
        functools.partial(_ffn_kernel, post=post),
        grid=(n // tm, dff // tf),
        in_specs=[row, vec,
                  _layer_spec((d, tf), layer, lambda i, j: (0, j)),
                  _layer_spec((d, tf), layer, lambda i, j: (0, j)),
                  _layer_spec((tf, d), layer, lambda i, j: (j, 0)),
                  vec],
        out_specs=out_specs,
        out_shape=out_shape,
        scratch_shapes=[pltpu.VMEM((tm, d), BF16)],
        compiler_params=_params("parallel", "arbitrary"),
        name="ffn_post" if post else "ffn",
    )(x, g, wg, wu, wd, g2)
    return res[0] if post else res


def _bias_slots(f, axis, *, query):
    hi, mid, lo = (p.astype(F32) for p in _split3(f))
    shape = (LANES, f.shape[1]) if axis == 0 else (f.shape[0], LANES)
    slot = lax.broadcasted_iota(jnp.int32, shape, axis)
    if not query:
        hi, mid, lo = -hi, -mid, -lo
        slot = slot - 3
    vals = jnp.where(slot == 0, hi, jnp.where(slot == 1, mid, jnp.where(slot == 2, lo, 0.0)))
    ones = (slot >= 3) & (slot < 6) if query else (slot >= -3) & (slot < 0)
    return jnp.where(ones, 1.0, vals)


def _fox_q_kernel(*refs, heads, hd, with_bias, qscale):
    if with_bias:
        h_ref, w_ref, g_ref, f_ref, o_ref = refs
    else:
        h_ref, w_ref, g_ref, o_ref = refs
    zt = _dot_nt(w_ref[...], h_ref[...])
    g = jnp.broadcast_to(g_ref[...], (hd, zt.shape[1]))
    for hh in range(heads):
        qn = _rms(zt[hh * hd:(hh + 1) * hd, :], g, axis=0) * qscale
        if with_bias:
            bias = _bias_slots(f_ref[hh:hh + 1, :] * LOG2E, 0, query=True)
            o_ref[hh] = jnp.concatenate([qn, bias], axis=0).astype(BF16)
        else:
            o_ref[hh] = qn.astype(BF16)


def _fox_q(h, wqt, g, fcum_t, layer, *, heads, hd, prompt, tm):
    n, d = h.shape
    ins = [h, wqt, g]
    in_specs = [pl.BlockSpec((tm, d), lambda i: (i, 0)),
                _layer_spec((heads * hd, d), layer, lambda i: (0, 0)),
                _layer_spec((hd, 1), layer, lambda i: (0, 0))]
    if prompt:
        ins.append(fcum_t)
        in_specs.append(pl.BlockSpec((heads, tm), lambda i: (0, i)))
    dk = hd + LANES if prompt else hd
    return pl.pallas_call(
        functools.partial(_fox_q_kernel, heads=heads, hd=hd, with_bias=prompt,
                          qscale=float(hd) ** -0.5 * LOG2E if prompt else 1.0),
        grid=(n // tm,),
        in_specs=in_specs,
        out_specs=pl.BlockSpec((heads, dk, tm), lambda i: (0, 0, i)),
        out_shape=jax.ShapeDtypeStruct((heads, dk, n), BF16),
        compiler_params=_params("parallel"),
        name="fox_q",
    )(*ins)


def _fox_kv_kernel(*refs, mode, heads, hd, with_bias, tk):
    it = iter(refs)
    h_ref, w_ref = next(it), next(it)
    g_ref = next(it) if mode == "k" else None
    f_ref = next(it) if with_bias else None
    next(it)
    state_ref = next(it)
    o_ref = next(it, None)
    z = _dot(h_ref[...], w_ref[...])
    if with_bias:
        fs = f_ref[...] * LOG2E
    outs = []
    for hh in range(heads):
        zh = z[:, hh * hd:(hh + 1) * hd]
        if mode == "k":
            zh = _rms(zh, g_ref[...])
        outs.append(zh)
        if o_ref is None:
            continue
        if mode == "v":
            for c in range(zh.shape[0] // tk):
                o_ref[hh, c] = zh[c * tk:(c + 1) * tk, :].T.astype(BF16)
        else:
            o_ref[hh] = jnp.concatenate([zh, _bias_slots(fs[:, hh:hh + 1], 1, query=False)], axis=1).astype(BF16)
    state_ref[...] = jnp.concatenate(outs, axis=1)


def _fox_kv(h, w, g, fcum, state, layer, *, mode, heads, hd, prompt, tm, tk):
    n, d = h.shape
    width = heads * hd
    with_bias = prompt and mode == "k"
    row = lambda c: pl.BlockSpec((tm, c), lambda i: (i, 0))
    ins, in_specs = [h, w], [row(d), _layer_spec((d, width), layer, lambda i: (0, 0))]
    if mode == "k":
        ins.append(g)
        in_specs.append(_layer_spec((1, hd), layer, lambda i: (0, 0)))
    if with_bias:
        ins.append(fcum)
        in_specs.append(row(heads))
    aliases = {len(ins): 0}
    ins.append(state)
    in_specs.append(pl.BlockSpec(memory_space=pl.ANY))
    out_shape = [jax.ShapeDtypeStruct(state.shape, F32)]
    out_specs = [pl.BlockSpec((None, tm, width), lambda i: (layer, i, 0))]
    if prompt and mode == "v":
        out_shape.append(jax.ShapeDtypeStruct((heads, n // tk, hd, tk), BF16))
        out_specs.append(pl.BlockSpec((heads, tm // tk, hd, tk), lambda i: (0, i, 0, 0)))
    elif with_bias:
        out_shape.append(jax.ShapeDtypeStruct((heads, n, hd + LANES), BF16))
        out_specs.append(pl.BlockSpec((heads, tm, hd + LANES), lambda i: (0, i, 0)))
    return pl.pallas_call(
        functools.partial(_fox_kv_kernel, mode=mode, heads=heads, hd=hd, with_bias=with_bias, tk=tk),
        grid=(n // tm,),
        in_specs=in_specs,
        out_specs=out_specs,
        out_shape=out_shape,
        input_output_aliases=aliases,
        compiler_params=_params("parallel"),
        name="fox_" + mode,
    )(*ins)


def _mla_q_kernel(h_ref, wcq_ref, gq_ref, wuq_ref, gh_ref, cos_ref, sin_ref, o_ref, cq_scr, *, hg, nope, rope, qscale):
    @pl.when(pl.program_id(1) == 0)
    def _():
        cq_scr[...] = _rms(_dot_nt(wcq_ref[...], h_ref[...]), gq_ref[...], axis=0).astype(BF16)

    zt = _dot(wuq_ref[...], cq_scr[...])
    tm = zt.shape[1]
    gh = gh_ref[...]
    g_n = jnp.broadcast_to(gh[:nope], (nope, tm))
    g_r = jnp.broadcast_to(gh[nope:nope + rope], (rope, tm))
    g_s = jnp.broadcast_to(gh[nope + rope:], (rope, tm))
    cos, sin = cos_ref[...], sin_ref[...]
    pad = jnp.zeros((LANES - rope, tm), F32)
    per = nope + LANES
    for hh in range(hg):
        base = hh * per
        qn = _rms(zt[base:base + nope, :], g_n, axis=0)
        r = zt[base + nope:base + nope + rope, :]
        inv = lax.rsqrt(jnp.mean(r * r, axis=0, keepdims=True) + NORM_EPS)
        rot = (r * inv * g_r) * cos + (zt[base + nope + rope:base + per, :] * inv * g_s) * sin
        o_ref[hh] = jnp.concatenate([qn * qscale, rot * qscale, pad], axis=0).astype(BF16)


def _mla_q(h, wcqt, gq, wuqt, gh, cos_t, sin_t, layer, *, heads, nope, rope, qscale, tm, hg):
    n, d = h.shape
    ql = wcqt.shape[1]
    per = nope + LANES
    return pl.pallas_call(
        functools.partial(_mla_q_kernel, hg=hg, nope=nope, rope=rope, qscale=qscale),
        grid=(n // tm, heads // hg),
        in_specs=[pl.BlockSpec((tm, d), lambda i, j: (i, 0)),
                  _layer_spec((ql, d), layer, lambda i, j: (0, 0)),
                  _layer_spec((ql, 1), layer, lambda i, j: (0, 0)),
                  _layer_spec((hg * per, ql), layer, lambda i, j: (j, 0)),
                  _layer_spec((per, 1), layer, lambda i, j: (0, 0)),
                  pl.BlockSpec((rope, tm), lambda i, j: (0, i)),
                  pl.BlockSpec((rope, tm), lambda i, j: (0, i))],
        out_specs=pl.BlockSpec((hg, per, tm), lambda i, j: (j, 0, i)),
        out_shape=jax.ShapeDtypeStruct((heads, per, n), BF16),
        scratch_shapes=[pltpu.VMEM((ql, tm), BF16)],
        compiler_params=_params("parallel", "arbitrary"),
        name="mla_q",
    )(h, wcqt, gq, wuqt, gh, cos_t, sin_t)


def _rotary(y, cos, sin):
    return y * cos + pltpu.roll(y, LANES // 2, 1) * sin


def _log_sigmoid(x):
    return jnp.minimum(x, 0.0) - jnp.log1p(jnp.exp(-jnp.abs(x)))


def _mla_kv_kernel(*refs, up, hg, kvl, rope, fh, nope, vdim, tk):
    it = iter(refs)
    h_ref, w6_ref, gkv_ref, gk_ref, b_ref, cos_ref, sin_ref = (next(it) for _ in range(7))
    if up:
        wuk_ref, wuvt_ref, gkn_ref = next(it), next(it), next(it)
    for _ in range(3):
        next(it)
    ckv_o, kpe_o, logf_o = next(it), next(it), next(it)
    if up:
        kcat_o, vt_o, ckv_scr, kpe_scr = (next(it) for _ in range(4))

    @pl.when(pl.program_id(1) == 0)
    def _():
        z = _dot(h_ref[...], w6_ref[...])
        ckv = _rms(z[:, :kvl], gkv_ref[...])
        ckv_o[...] = ckv
        rot = _rotary(_rms(z[:, kvl:kvl + LANES], gk_ref[...]), cos_ref[...], sin_ref[...])
        kpe_o[...] = rot[:, :rope]
        logf_o[...] = _log_sigmoid(z[:, kvl + LANES:] + b_ref[...])[:, :fh]
        if up:
            ckv_scr[...] = ckv.astype(BF16)
            kpe_scr[...] = rot.astype(BF16)

    if up:
        ckv = ckv_scr[...]
        kn = _dot(ckv, wuk_ref[...])
        vt = _dot_nt(wuvt_ref[...], ckv)
        for hh in range(hg):
            knh = _rms(kn[:, hh * nope:(hh + 1) * nope], gkn_ref[...]).astype(BF16)
            kcat_o[hh] = jnp.concatenate([knh, kpe_scr[...]], axis=1)
            for c in range(vt.shape[1] // tk):
                vt_o[hh, c] = vt[hh * vdim:(hh + 1) * vdim, c * tk:(c + 1) * tk].astype(BF16)


def _mla_kv(h, w6, gkv, gk, b, cos, sin, wuk, wuvt, gkn, states, layer, *, up, heads, nope, vdim, rope, fh, tm, tk, hg):
    n, d = h.shape
    kvl = gkv.shape[2]
    nj = heads // hg if up else 1
    c2 = lambda shape: _layer_spec(shape, layer, lambda i, j: (0,) * len(shape))
    ins = [h, w6, gkv, gk, b, cos, sin]
    in_specs = [pl.BlockSpec((tm, d), lambda i, j: (i, 0)), c2(w6.shape[1:]), c2((1, kvl)), c2((1, LANES)),
                c2((1, LANES)), pl.BlockSpec((tm, LANES), lambda i, j: (i, 0)),
                pl.BlockSpec((tm, LANES), lambda i, j: (i, 0))]
    if up:
        ins += [wuk, wuvt, gkn]
        in_specs += [_layer_spec((kvl, hg * nope), layer, lambda i, j: (0, j)),
                     _layer_spec((hg * vdim, kvl), layer, lambda i, j: (j, 0)), c2((1, nope))]
    aliases = {len(ins) + k: k for k in range(3)}
    ins += list(states)
    in_specs += [pl.BlockSpec(memory_space=pl.ANY)] * 3
    out_shape = [jax.ShapeDtypeStruct(s.shape, F32) for s in states]
    out_specs = [pl.BlockSpec((None, tm, w), lambda i, j: (layer, i, 0)) for w in (kvl, rope, fh)]
    scratch = []
    if up:
        out_shape += [jax.ShapeDtypeStruct((heads, n, nope + LANES), BF16),
                      jax.ShapeDtypeStruct((heads, n // tk, vdim, tk), BF16)]
        out_specs += [pl.BlockSpec((hg, tm, nope + LANES), lambda i, j: (j, i, 0)),
                      pl.BlockSpec((hg, tm // tk, vdim, tk), lambda i, j: (j, i, 0, 0))]
        scratch = [pltpu.VMEM((tm, kvl), BF16), pltpu.VMEM((tm, LANES), BF16)]
    return pl.pallas_call(
        functools.partial(_mla_kv_kernel, up=up, hg=hg, kvl=kvl, rope=rope, fh=fh, nope=nope, vdim=vdim, tk=tk),
        grid=(n // tm, nj),
        in_specs=in_specs,
        out_specs=out_specs,
        out_shape=out_shape,
        input_output_aliases=aliases,
        scratch_shapes=scratch,
        compiler_params=_params("parallel", "arbitrary"),
        name="mla_kv",
    )(*ins)


def _cumsum_kernel(x_ref, tri_ref, o_ref):
    tri = tri_ref[...]
    carry = jnp.zeros((x_ref.shape[0], 1), F32)
    for c in range(x_ref.shape[1] // CUMSUM_BLOCK):
        sl = slice(c * CUMSUM_BLOCK, (c + 1) * CUMSUM_BLOCK)
        hi, mid, lo = _split3(x_ref[:, sl])
        blk = _dot(hi, tri) + _dot(mid, tri) + _dot(lo, tri) + carry
        o_ref[:, sl] = blk
        carry = blk[:, CUMSUM_BLOCK - 1:]


def _cumsum_time(x):
    b, hh, s = x.shape
    r = lax.broadcasted_iota(jnp.int32, (CUMSUM_BLOCK, CUMSUM_BLOCK), 0)
    c = lax.broadcasted_iota(jnp.int32, (CUMSUM_BLOCK, CUMSUM_BLOCK), 1)
    tri = (r <= c).astype(BF16)
    return pl.pallas_call(
        _cumsum_kernel,
        grid=(b,),
        in_specs=[pl.BlockSpec((None, hh, s), lambda i: (i, 0, 0)),
                  pl.BlockSpec((CUMSUM_BLOCK, CUMSUM_BLOCK), lambda i: (0, 0))],
        out_specs=pl.BlockSpec((None, hh, s), lambda i: (i, 0, 0)),
        out_shape=jax.ShapeDtypeStruct(x.shape, F32),
        compiler_params=_params("parallel"),
        name="cumsum_logf",
    )(x, tri)


def _attn_prompt_kernel(qt_ref, k_ref, vt_ref, o_ref, sa_scr, mxa_scr, sb_scr, mxb_scr, m_scr, l_scr, acc_scr,
                        *, hg, tq, chunk):
    i = pl.program_id(2)
    m_scr[...] = jnp.full_like(m_scr, MASKED)
    l_scr[...] = jnp.zeros_like(l_scr)
    acc_scr[...] = jnp.zeros_like(acc_scr)
    bufs = {"a": (sa_scr, mxa_scr), "b": (sb_scr, mxb_scr)}

    def stash(j, buf, diagonal=False):
        s_ref, mx_ref = bufs[buf]
        start = pl.multiple_of(j * tq, tq)
        for hh in range(hg):
            t = _dot(k_ref[hh, pl.ds(start, tq), :], qt_ref[hh])
            if diagonal:
                ks = lax.broadcasted_iota(jnp.int32, t.shape, 0)
                qs = lax.broadcasted_iota(jnp.int32, t.shape, 1)
                vis = (ks // chunk) <= (qs // chunk) if chunk > 1 else ks <= qs
                t = jnp.where(vis, t, MASKED)
            s_ref[hh] = t
            mx_ref[hh] = jnp.max(t, axis=0, keepdims=True)

    def absorb(j, buf):
        s_ref, mx_ref = bufs[buf]
        for hh in range(hg):
            m_old = m_scr[hh]
            m_new = jnp.maximum(m_old, mx_ref[hh])
            alpha = jnp.exp2(m_old - m_new)
            p = jnp.exp2(s_ref[hh] - m_new)
            l_scr[hh] = alpha * l_scr[hh] + jnp.sum(p, axis=0, keepdims=True)
            acc_scr[hh] = alpha * acc_scr[hh] + _dot(vt_ref[hh, j], p.astype(BF16))
            m_scr[hh] = m_new

    lead = (i + 1) % 2

    @pl.when((i >= 1) & (lead == 0))
    def _():
        stash(0, "a")

    @pl.when((i >= 1) & (lead == 1))
    def _():
        stash(0, "b")
        stash(1, "a")
        absorb(0, "b")

    def body(jj, carry):
        t0 = lead + 2 * jj
        stash(t0 + 1, "b")
        absorb(t0, "a")
        stash(t0 + 2, "a")
        absorb(t0 + 1, "b")
        return carry

    lax.fori_loop(0, (i - 1 - lead) // 2, body, 0)

    @pl.when(i >= 1)
    def _():
        stash(i, "b", diagonal=True)
        absorb(i - 1, "a")

    @pl.when(i == 0)
    def _():
        stash(i, "b", diagonal=True)

    absorb(i, "b")
    o_ref[...] = jnp.concatenate([(acc_scr[hh] / l_scr[hh]).T for hh in range(hg)], axis=1).astype(o_ref.dtype)


def _attn_prompt(qt, k, vt, *, batch, seq, tq, chunk, hg):
    heads, dk, n = qt.shape
    dv = vt.shape[2]
    nq = seq // tq
    return pl.pallas_call(
        functools.partial(_attn_prompt_kernel, hg=hg, tq=tq, chunk=chunk),
        grid=(batch, heads // hg, nq),
        in_specs=[pl.BlockSpec((hg, dk, tq), lambda b, g, i: (g, 0, b * nq + i)),
                  pl.BlockSpec((hg, seq, dk), lambda b, g, i: (g, b, 0)),
                  pl.BlockSpec((hg, nq, dv, tq), lambda b, g, i: (g, b, 0, 0))],
        out_specs=pl.BlockSpec((tq, hg * dv), lambda b, g, i: (b * nq + i, g)),
        out_shape=jax.ShapeDtypeStruct((n, heads * dv), BF16),
        scratch_shapes=[pltpu.VMEM((hg, tq, tq), F32), pltpu.VMEM((hg, 1, tq), F32),
                        pltpu.VMEM((hg, tq, tq), F32), pltpu.VMEM((hg, 1, tq), F32),
                        pltpu.VMEM((hg, 1, tq), F32), pltpu.VMEM((hg, 1, tq), F32), pltpu.VMEM((hg, dv, tq), F32)],
        compiler_params=_params("parallel", "parallel", "arbitrary"),
        name="attn_prompt_chunk%d" % chunk,
    )(qt, k, vt)


def _softmax_two_segments(s_c, s_n, v_c, v_n, transposed_v=False):
    m = jnp.maximum(jnp.max(s_c, axis=1, keepdims=True), jnp.max(s_n, axis=1, keepdims=True))
    p_c = jnp.exp(s_c - m)
    p_n = jnp.exp(s_n - m)
    l = jnp.sum(p_c, axis=1, keepdims=True) + jnp.sum(p_n, axis=1, keepdims=True)
    pv = _dot_nt if transposed_v else _dot
    return (pv(p_c.astype(BF16), v_c) + pv(p_n.astype(BF16), v_n)) / l


def _fox_sample_kernel(q_ref, kc_ref, vc_ref, kn_ref, vn_ref, fq_ref, fkc_ref, fkn_ref, o_ref, *, heads, hd, scale):
    t = q_ref.shape[0]
    rows = lax.broadcasted_iota(jnp.int32, (t, t), 0)
    cols = lax.broadcasted_iota(jnp.int32, (t, t), 1)
    outs = []
    for hh in range(heads):
        sl = slice(hh * hd, (hh + 1) * hd)
        q = q_ref[:, sl]
        fq = fq_ref[hh]
        rows_h = pl.ds(hh, kc_ref.shape[0] // heads, stride=heads)
        s_c = _dot_nt(q, kc_ref[rows_h, :].astype(BF16)) * scale + (fq - fkc_ref[hh])
        s_n = _dot_nt(q, kn_ref[:, sl].astype(BF16)) * scale + (fq - fkn_ref[hh])
        s_n = jnp.where(cols <= rows, s_n, MASKED)
        outs.append(_softmax_two_segments(s_c, s_n, vc_ref[rows_h, :].astype(BF16), vn_ref[:, sl].astype(BF16)))
    o_ref[...] = jnp.concatenate(outs, axis=1).astype(o_ref.dtype)


def _fox_sample(q, kc, vc, k_state, v_state, layer, fq, fkc, fkn):
    n, width = q.shape
    depth, b, s, heads, hd = kc.shape
    t = n // b
    kc, vc = (a.reshape(depth, b, s * heads, hd) for a in (kc, vc))
    cache = pl.BlockSpec((None, None, s * heads, hd), lambda i: (layer, i, 0, 0))
    new = pl.BlockSpec((None, t, width), lambda i: (layer, i, 0))
    return pl.pallas_call(
        functools.partial(_fox_sample_kernel, heads=heads, hd=hd, scale=float(hd) ** -0.5),
        grid=(b,),
        in_specs=[pl.BlockSpec((t, width), lambda i: (i, 0)), cache, cache, new, new,
                  pl.BlockSpec((None, heads, t, 1), lambda i: (i, 0, 0, 0)),
                  pl.BlockSpec((None, heads, 1, s), lambda i: (i, 0, 0, 0)),
                  pl.BlockSpec((None, heads, 1, t), lambda i: (i, 0, 0, 0))],
        out_specs=pl.BlockSpec((t, width), lambda i: (i, 0)),
        out_shape=jax.ShapeDtypeStruct((n, width), BF16),
        compiler_params=_params("parallel"),
        name="fox_sample",
    )(q, kc, vc, k_state, v_state, fq, fkc, fkn)


def _mla_sample_kernel(q_ref, cc_ref, pc_ref, cn_ref, pn_ref, wt_ref, gkn_ref, o_ref, cc_scr, pct_scr,
                       *, hg, nope, past, scale, chunk):
    t = q_ref.shape[1]
    rows = lax.broadcasted_iota(jnp.int32, (t, t), 0) + past
    cols = lax.broadcasted_iota(jnp.int32, (t, t), 1) + past
    vis = (cols // chunk) <= (rows // chunk)
    per = wt_ref.shape[0] // hg

    def rope_rows(p_ref):
        pe = p_ref[...]
        pad = jnp.zeros((pe.shape[0], LANES - pe.shape[1]), F32)
        return jnp.concatenate([pe, pad], axis=1).T.astype(BF16)

    @pl.when(pl.program_id(1) == 0)
    def _():
        cc_scr[...] = cc_ref[...].astype(BF16)
        pct_scr[...] = rope_rows(pc_ref)

    pnt = rope_rows(pn_ref)
    kvt_c = _dot_nt(wt_ref[...], cc_scr[...])
    kvt_n = _dot_nt(wt_ref[...], cn_ref[...].astype(BF16))
    g = gkn_ref[...]
    outs = []
    for hh in range(hg):
        q = q_ref[hh]

        def keys_t(kvt, pet):
            kn = _rms(kvt[hh * per:hh * per + nope, :], g, axis=0).astype(BF16)
            return jnp.concatenate([kn, pet], axis=0)

        s_c = _dot(q, keys_t(kvt_c, pct_scr[...])) * scale
        s_n = jnp.where(vis, _dot(q, keys_t(kvt_n, pnt)) * scale, MASKED)
        vt_c = kvt_c[hh * per + nope:(hh + 1) * per, :].astype(BF16)
        vt_n = kvt_n[hh * per + nope:(hh + 1) * per, :].astype(BF16)
        outs.append(_softmax_two_segments(s_c, s_n, vt_c, vt_n, transposed_v=True))
    o_ref[...] = jnp.concatenate(outs, axis=1).astype(o_ref.dtype)


def _mla_sample(q, ckv_cache, kpe_cache, ckv_state, kpe_state, layer, wukvt, gkn, *, heads, nope, vdim, hg, chunk):
    _, n, dk = q.shape
    _, b, s, kvl = ckv_cache.shape
    rope = kpe_cache.shape[3]
    t = n // b
    per = nope + vdim
    return pl.pallas_call(
        functools.partial(_mla_sample_kernel, hg=hg, nope=nope, past=s, scale=float(nope + rope) ** -0.5, chunk=chunk),
        grid=(b, heads // hg),
        in_specs=[pl.BlockSpec((hg, t, dk), lambda i, g: (g, i, 0)),
                  pl.BlockSpec((None, None, s, kvl), lambda i, g: (layer, i, 0, 0)),
                  pl.BlockSpec((None, None, s, rope), lambda i, g: (layer, i, 0, 0)),
                  pl.BlockSpec((None, t, kvl), lambda i, g: (layer, i, 0)),
                  pl.BlockSpec((None, t, rope), lambda i, g: (layer, i, 0)),
                  _layer_spec((hg * per, kvl), layer, lambda i, g: (g, 0)),
                  _layer_spec((nope, 1), layer, lambda i, g: (0, 0))],
        out_specs=pl.BlockSpec((t, hg * vdim), lambda i, g: (i, g)),
        out_shape=jax.ShapeDtypeStruct((n, heads * vdim), BF16),
        scratch_shapes=[pltpu.VMEM((s, kvl), BF16), pltpu.VMEM((LANES, s), BF16)],
        compiler_params=_params("parallel", "arbitrary"),
        name="mla_sample",
    )(q, ckv_cache, kpe_cache, ckv_state, kpe_state, wukvt, gkn)


def _merge_kernel(x_ref, h_ref, of_ref, om_ref, wga_ref, wgb_ref, wa_ref, wb_ref, wo_ref, o_ref):
    j = pl.program_id(1)

    @pl.when(j == 0)
    def _():
        o_ref[...] = jnp.zeros_like(o_ref)

    h = h_ref[...]
    ga = jax.nn.sigmoid(_dot(h, wga_ref[...]))
    gb = jax.nn.sigmoid(_dot(h, wgb_ref[...]))
    m = ga * _dot(of_ref[...], wa_ref[...]) + gb * _dot(om_ref[...], wb_ref[...])
    o_ref[...] += _dot(m.astype(BF16), wo_ref[...])

    @pl.when(j == pl.num_programs(1) - 1)
    def _():
        o_ref[...] = x_ref[...] + o_ref[...]


def _merge(x, h, of, om, wga, wgb, wa, wb, wo, layer, *, tm, tn):
    n, d = x.shape
    row = lambda c: pl.BlockSpec((tm, c), lambda i, j: (i, 0))
    col = lambda r: _layer_spec((r, tn), layer, lambda i, j: (0, j))
    return pl.pallas_call(
        _merge_kernel,
        grid=(n // tm, d // tn),
        in_specs=[row(d), row(d), row(of.shape[1]), row(om.shape[1]),
                  col(d), col(d), col(of.shape[1]), col(om.shape[1]),
                  _layer_spec((tn, d), layer, lambda i, j: (j, 0))],
        out_specs=row(d),
        out_shape=jax.ShapeDtypeStruct((n, d), F32),
        compiler_params=_params("parallel", "arbitrary"),
        name="merge_out",
    )(x, h, of, om, wga, wgb, wa, wb, wo)


def _swap_halves(a):
    half = a.shape[-1] // 2
    return jnp.concatenate([a[..., half:], a[..., :half]], axis=-1)


def _rope_angles(pos, rope):
    half = rope // 2
    inv = ROPE_THETA ** (-jnp.arange(half, dtype=F32) / half)
    ang = pos.astype(F32)[:, None] * inv[None, :]
    return jnp.cos(ang), jnp.sin(ang)


def kernel(x_prompt, x_sample, cache_fox_k, cache_fox_v, cache_fox_logf, cache_mla_ckv, cache_mla_kpe, ffn1_norm, ffn1_w_gate, ffn1_w_up, ffn1_w_down, mix_norm, w_in, b_forget, fox_q_norm, fox_k_norm, mla_q_lat_norm, w_uq, mla_q_nope_norm, mla_q_rope_norm, mla_kv_lat_norm, w_ukv, mla_k_nope_norm, mla_k_rope_norm, w_branch_a, w_branch_b, w_out, ffn2_norm, ffn2_w_gate, ffn2_w_up, ffn2_w_down, post_norm):
    depth, d = ffn1_norm.shape
    bp, tp, _ = x_prompt.shape
    bs, ts, _ = x_sample.shape
    past, fh, hd = cache_fox_k.shape[2], cache_fox_k.shape[3], cache_fox_k.shape[4]
    fw = fh * hd
    ql, kvl = mla_q_lat_norm.shape[1], mla_kv_lat_norm.shape[1]
    nope, rope = mla_q_nope_norm.shape[1], mla_q_rope_norm.shape[1]
    mh = w_uq.shape[2] // (nope + rope)
    vdim = w_ukv.shape[2] // mh - nope
    assert nope == LANES and 2 * rope == LANES and hd == LANES and vdim == LANES
    np_, ns = bp * tp, bs * ts

    bf = lambda a: a.astype(BF16)
    tr = lambda a: jnp.swapaxes(a, 1, 2)
    o = [0]
    for wdt in (fw, fw, fw, fh, ql, kvl, rope, d, d):
        o.append(o[-1] + wdt)
    w_qt = bf(tr(w_in[:, :, o[0]:o[1]]))
    w_k, w_v = bf(w_in[:, :, o[1]:o[2]]), bf(w_in[:, :, o[2]:o[3]])
    w_f, w_cq, w_ckv, w_kpe = (w_in[:, :, o[k]:o[k + 1]] for k in range(3, 7))
    w_ga, w_gb = bf(w_in[:, :, o[7]:o[8]]), bf(w_in[:, :, o[8]:o[9]])
    w_cqt = bf(tr(w_cq))
    w6 = bf(jnp.concatenate([w_ckv, w_kpe, _swap_halves(w_kpe), w_f,
                             jnp.zeros((depth, d, LANES - fh), F32)], axis=2))
    b6 = jnp.concatenate([b_forget, jnp.zeros((depth, LANES - fh), F32)], axis=1)
    uq = w_uq.reshape(depth, ql, mh, nope + rope)
    wuqt = bf(tr(jnp.concatenate([uq, _swap_halves(uq[..., nope:])], axis=3).reshape(depth, ql, mh * (nope + LANES))))
    g_qh = jnp.concatenate([mla_q_nope_norm, mla_q_rope_norm, _swap_halves(mla_q_rope_norm)], axis=1)
    g_kr = jnp.concatenate([mla_k_rope_norm, _swap_halves(mla_k_rope_norm)], axis=1)
    ukv = w_ukv.reshape(depth, kvl, mh, nope + vdim)
    wuk = bf(ukv[..., :nope].reshape(depth, kvl, mh * nope))
    wuvt = bf(tr(ukv[..., nope:].reshape(depth, kvl, mh * vdim)))
    wukvt = bf(tr(w_ukv))
    f1g, f1u, f1d = bf(ffn1_w_gate), bf(ffn1_w_up), bf(ffn1_w_down)
    f2g, f2u, f2d = bf(ffn2_w_gate), bf(ffn2_w_up), bf(ffn2_w_down)
    wa, wb, wo = bf(w_branch_a), bf(w_branch_b), bf(w_out)
    rowv = lambda a: a[:, None, :]
    colv = lambda a: a[:, :, None]

    def rope_tables(pos, reps):
        cos, sin = _rope_angles(pos, rope)
        pad = jnp.zeros((pos.shape[0], LANES - rope), F32)
        lanes = (jnp.concatenate([cos, cos, pad], axis=1), jnp.concatenate([-sin, sin, pad], axis=1))
        subl = (jnp.concatenate([cos, cos], axis=1).T, jnp.concatenate([-sin, sin], axis=1).T)
        return [jnp.tile(a, (reps, 1)) for a in lanes] + [jnp.tile(a, (1, reps)) for a in subl]

    rope_p = rope_tables(jnp.arange(tp, dtype=jnp.int32), bp)
    rope_s = rope_tables(past + jnp.arange(ts, dtype=jnp.int32), bs)

    def states(n):
        return [jnp.zeros((depth, n, w), F32) for w in (fw, fw, fh, kvl, rope)]

    st_p, st_s = states(np_), states(ns)

    tq = _tile(tp, ATTN_TILE)
    tm_p, tm_s = _tile(np_, 512), _tile(ns, 512)
    assert tm_p % tq == 0 and tp % tq == 0
    tf = _tile(f1g.shape[2], 512)
    tn = _tile(d, 512)
    s_pad = -(-(past + ts) // CUMSUM_BLOCK) * CUMSUM_BLOCK
    fox_scale, mla_scale = float(hd) ** -0.5, float(nope + rope) ** -0.5

    def tokens(x, st, l, *, prompt):
        tm = tm_p if prompt else tm_s
        cos, sin, cos_t, sin_t = rope_p if prompt else rope_s
        n = x.shape[0]
        x1, h = _ffn(x, rowv(ffn1_norm), f1g, f1u, f1d, rowv(mix_norm), l, post=False, tm=tm, tf=tf)
        res = _mla_kv(h, w6, rowv(mla_kv_lat_norm), rowv(g_kr), rowv(b6), cos, sin, wuk, wuvt,
                      rowv(mla_k_nope_norm), (st[3], st[4], st[2]), l, up=prompt, heads=mh, nope=nope,
                      vdim=vdim, rope=rope, fh=fh, tm=tm, tk=tq, hg=4)
        st[3], st[4], st[2] = res[:3]
        q_m = _mla_q(h, w_cqt, colv(mla_q_lat_norm), wuqt, colv(g_qh), cos_t, sin_t, l, heads=mh, nope=nope,
                     rope=rope, qscale=mla_scale * LOG2E if prompt else 1.0, tm=tm, hg=4)
        kv = functools.partial(_fox_kv, heads=fh, hd=hd, prompt=prompt, tm=tm, tk=tq)
        if prompt:
            logf_t = jnp.swapaxes(st[2][l].reshape(bp, tp, fh), 1, 2)
            fcum_t = _cumsum_time(logf_t)
            fcum = jnp.swapaxes(fcum_t, 1, 2).reshape(n, fh)
            fcum_t = jnp.swapaxes(fcum_t, 0, 1).reshape(fh, n)
            q_f = _fox_q(h, w_qt, colv(fox_q_norm), fcum_t, l, heads=fh, hd=hd, prompt=True, tm=tm)
            st[0], k_f = kv(h, w_k, rowv(fox_k_norm), fcum, st[0], l, mode="k")
            st[1], vt_f = kv(h, w_v, None, None, st[1], l, mode="v")
            o_f = _attn_prompt(q_f, k_f, vt_f, batch=bp, seq=tp, tq=tq, chunk=1, hg=ATTN_HEADS)
            o_m = _attn_prompt(q_m, res[3], res[4], batch=bp, seq=tp, tq=tq, chunk=CHUNK, hg=ATTN_HEADS)
        else:
            q_f = _fox_q(h, w_qt, colv(fox_q_norm), None, l, heads=fh, hd=hd, prompt=False, tm=tm)
            q_f = jnp.transpose(q_f, (2, 0, 1)).reshape(n, fw)
            (st[0],) = kv(h, w_k, rowv(fox_k_norm), None, st[0], l, mode="k")
            (st[1],) = kv(h, w_v, None, None, st[1], l, mode="v")
            logf_all = jnp.concatenate([cache_fox_logf[l], st[2][l].reshape(bs, ts, fh),
                                        jnp.zeros((bs, s_pad - past - ts, fh), F32)], axis=1)
            fall = _cumsum_time(jnp.swapaxes(logf_all, 1, 2))
            fkc = fall[:, :, None, :past]
            fkn = fall[:, :, None, past:past + ts]
            fq = fall[:, :, past:past + ts, None]
            o_f = _fox_sample(q_f, cache_fox_k, cache_fox_v, st[0], st[1], l, fq, fkc, fkn)
            o_m = _mla_sample(jnp.swapaxes(q_m, 1, 2), cache_mla_ckv, cache_mla_kpe, st[3], st[4], l, wukvt,
                              colv(mla_k_nope_norm), heads=mh, nope=nope, vdim=vdim, hg=4, chunk=CHUNK)
        x2 = _merge(x1, h, o_f, o_m, w_ga, w_gb, wa, wb, wo, l, tm=tm, tn=tn)
        return _ffn(x2, rowv(ffn2_norm), f2g, f2u, f2d, rowv(post_norm), l, post=True, tm=tm, tf=tf)

    y_p, y_s = x_prompt.reshape(np_, d), x_sample.reshape(ns, d)
    for l in range(depth):
        y_p = tokens(y_p, st_p, l, prompt=True)
        y_s = tokens(y_s, st_s, l, prompt=False)

    def shaped(st, b, t):
        return (st[0].reshape(depth, b, t, fh, hd), st[1].reshape(depth, b, t, fh, hd), st[2].reshape(depth, b, t, fh),
                st[3].reshape(depth, b, t, kvl), st[4].reshape(depth, b, t, rope))

    return (y_p.reshape(bp, tp, d), y_s.reshape(bs, ts, d)) + shaped(st_p, bp, tp) + shaped(st_s, bs, ts)
```

```python
import functools
import math

import jax
import jax.numpy as jnp
from jax import lax
from jax.experimental import pallas as pl
from jax.experimental.pallas import tpu as pltpu

F32 = jnp.float32
BF16 = jnp.bfloat16

NORM_EPS = 1e-6
CHUNK = 64
ROPE_THETA = 10000.0
MASKED = -1e30
LOG2E = math.log2(math.e)

LANES = 128
MXU_DIM = 256
VMEM_LIMIT_BYTES = 56 * 1024 * 1024
CUMSUM_BLOCK = MXU_DIM
ATTN_TILE = 512
ATTN_HEADS = 4


def _params(*semantics):
    return pltpu.CompilerParams(dimension_semantics=semantics, vmem_limit_bytes=VMEM_LIMIT_BYTES)


def _tile(n, pref):
    t = min(n, pref)
    while n % t:
        t -= 8
    return t


def _rms(x, g, axis=-1):
    return x * lax.rsqrt(jnp.mean(x * x, axis=axis, keepdims=True) + NORM_EPS) * g


def _dot(a, b):
    return jnp.dot(a, b, preferred_element_type=F32)


def _dot_nt(a, b):
    return lax.dot_general(a, b, (((1,), (1,)), ((), ())), preferred_element_type=F32)


def _split3(x):
    hi = x.astype(BF16)
    r = x - hi.astype(F32)
    mid = r.astype(BF16)
    lo = (r - mid.astype(F32)).astype(BF16)
    return hi, mid, lo


def _layer_spec(shape, layer, index_map):
    return pl.BlockSpec((None,) + tuple(shape), lambda *ids: (layer,) + tuple(index_map(*ids)))


def _ffn_kernel(x_ref, g_ref, wg_ref, wu_ref, wd_ref, g2_ref, o_ref, *rest, post):
    if post:
        (h_scr,) = rest
    else:
        h2_ref, h_scr = rest
    j = pl.program_id(1)

    @pl.when(j == 0)
    def _():
        h_scr[...] = _rms(x_ref[...], g_ref[...]).astype(BF16)
        o_ref[...] = jnp.zeros_like(o_ref)

    h = h_scr[...]
    g = _dot(h, wg_ref[...])
    u = _dot(h, wu_ref[...])
    a = (g * jax.nn.sigmoid(g) * u).astype(BF16)
    o_ref[...] += _dot(a, wd_ref[...])

    @pl.when(j == pl.num_programs(1) - 1)
    def _():
        y = x_ref[...] + 0.5 * o_ref[...]
        if post:
            o_ref[...] = _rms(y, g2_ref[...])
        else:
            o_ref[...] = y
            h2_ref[...] = _rms(y, g2_ref[...]).astype(BF16)


def _ffn(x, g, wg, wu, wd, g2, layer, *, post, tm, tf):
    n, d = x.shape
    dff = wg.shape[2]
    row = pl.BlockSpec((tm, d), lambda i, j: (i, 0))
    vec = _layer_spec((1, d), layer, lambda i, j: (0, 0))
    out_shape = [jax.ShapeDtypeStruct((n, d), F32)]
    out_specs = [row]
    if not post:
        out_shape.append(jax.ShapeDtypeStruct((n, d), BF16))
        out_specs.append(row)
    res = pl.pallas_call(
        functools.partial(_ffn_kernel, post=post),
        grid=(n // tm, dff // tf),
        in_specs=[row, vec,
                  _layer_spec((d, tf), layer, lambda i, j: (0, j)),
                  _layer_spec((d, tf), layer, lambda i, j: (0, j)),
                  _layer_spec((tf, d), layer, lambda i, j: (j, 0)),
                  vec],
        out_specs=out_specs,
        out_shape=out_shape,
        scratch_shapes=[pltpu.VMEM((tm, d), BF16)],
        compiler_params=_params("parallel", "arbitrary"),
        name="ffn_post" if post else "ffn",
    )(x, g, wg, wu, wd, g2)
    return res[0] if post else res


def _bias_slots(f, axis, *, query):
    hi, mid, lo = (p.astype(F32) for p in _split3(f))
    shape = (LANES, f.shape[1]) if axis == 0 else (f.shape[0], LANES)
    slot = lax.broadcasted_iota(jnp.int32, shape, axis)
    if not query:
        hi, mid, lo = -hi, -mid, -lo
        slot = slot - 3
    vals = jnp.where(slot == 0, hi, jnp.where(slot == 1, mid, jnp.where(slot == 2, lo, 0.0)))
    ones = (slot >= 3) & (slot < 6) if query else (slot >= -3) & (slot < 0)
    return jnp.where(ones, 1.0, vals)


def _fox_q_kernel(*refs, heads, hd, with_bias, qscale):
    if with_bias:
        h_ref, w_ref, g_ref, f_ref, o_ref = refs
    else:
        h_ref, w_ref, g_ref, o_ref = refs
    zt = _dot_nt(w_ref[...], h_ref[...])
    g = jnp.broadcast_to(g_ref[...], (hd, zt.shape[1]))
    for hh in range(heads):
        qn = _rms(zt[hh * hd:(hh + 1) * hd, :], g, axis=0) * qscale
        if with_bias:
            bias = _bias_slots(f_ref[hh:hh + 1, :] * LOG2E, 0, query=True)
            o_ref[hh] = jnp.concatenate([qn, bias], axis=0).astype(BF16)
        else:
            o_ref[hh] = qn.astype(BF16)


def _fox_q(h, wqt, g, fcum_t, layer, *, heads, hd, prompt, tm):
    n, d = h.shape
    ins = [h, wqt, g]
    in_specs = [pl.BlockSpec((tm, d), lambda i: (i, 0)),
                _layer_spec((heads * hd, d), layer, lambda i: (0, 0)),
                _layer_spec((hd, 1), layer, lambda i: (0, 0))]
    if prompt:
        ins.append(fcum_t)
        in_specs.append(pl.BlockSpec((heads, tm), lambda i: (0, i)))
    dk = hd + LANES if prompt else hd
    return pl.pallas_call(
        functools.partial(_fox_q_kernel, heads=heads, hd=hd, with_bias=prompt,
                          qscale=float(hd) ** -0.5 * LOG2E if prompt else 1.0),
        grid=(n // tm,),
        in_specs=in_specs,
        out_specs=pl.BlockSpec((heads, dk, tm), lambda i: (0, 0, i)),
        out_shape=jax.ShapeDtypeStruct((heads, dk, n), BF16),
        compiler_params=_params("parallel"),
        name="fox_q",
    )(*ins)


def _fox_kv_kernel(*refs, mode, heads, hd, with_bias, tk):
    it = iter(refs)
    h_ref, w_ref = next(it), next(it)
    g_ref = next(it) if mode == "k" else None
    f_ref = next(it) if with_bias else None
    next(it)
    state_ref = next(it)
    o_ref = next(it, None)
    z = _dot(h_ref[...], w_ref[...])
    if with_bias:
        fs = f_ref[...] * LOG2E
    outs = []
    for hh in range(heads):
        zh = z[:, hh * hd:(hh + 1) * hd]
        if mode == "k":
            zh = _rms(zh, g_ref[...])
        outs.append(zh)
        if o_ref is None:
            continue
        if mode == "v":
            for c in range(zh.shape[0] // tk):
                o_ref[hh, c] = zh[c * tk:(c + 1) * tk, :].T.astype(BF16)
        else:
            o_ref[hh] = jnp.concatenate([zh, _bias_slots(fs[:, hh:hh + 1], 1, query=False)], axis=1).astype(BF16)
    state_ref[...] = jnp.concatenate(outs, axis=1)


def _fox_kv(h, w, g, fcum, state, layer, *, mode, heads, hd, prompt, tm, tk):
    n, d = h.shape
    width = heads * hd
    with_bias = prompt and mode == "k"
    row = lambda c: pl.BlockSpec((tm, c), lambda i: (i, 0))
    ins, in_specs = [h, w], [row(d), _layer_spec((d, width), layer, lambda i: (0, 0))]
    if mode == "k":
        ins.append(g)
        in_specs.append(_layer_spec((1, hd), layer, lambda i: (0, 0)))
    if with_bias:
        ins.append(fcum)
        in_specs.append(row(heads))
    aliases = {len(ins): 0}
    ins.append(state)
    in_specs.append(pl.BlockSpec(memory_space=pl.ANY))
    out_shape = [jax.ShapeDtypeStruct(state.shape, F32)]
    out_specs = [pl.BlockSpec((None, tm, width), lambda i: (layer, i, 0))]
    if prompt and mode == "v":
        out_shape.append(jax.ShapeDtypeStruct((heads, n // tk, hd, tk), BF16))
        out_specs.append(pl.BlockSpec((heads, tm // tk, hd, tk), lambda i: (0, i, 0, 0)))
    elif with_bias:
        out_shape.append(jax.ShapeDtypeStruct((heads, n, hd + LANES), BF16))
        out_specs.append(pl.BlockSpec((heads, tm, hd + LANES), lambda i: (0, i, 0)))
    return pl.pallas_call(
        functools.partial(_fox_kv_kernel, mode=mode, heads=heads, hd=hd, with_bias=with_bias, tk=tk),
        grid=(n // tm,),
        in_specs=in_specs,
        out_specs=out_specs,
        out_shape=out_shape,
        input_output_aliases=aliases,
        compiler_params=_params("parallel"),
        name="fox_" + mode,
    )(*ins)


def _mla_q_kernel(h_ref, wcq_ref, gq_ref, wuq_ref, gh_ref, cos_ref, sin_ref, o_ref, cq_scr, *, hg, nope, rope, qscale):
    @pl.when(pl.program_id(1) == 0)
    def _():
        cq_scr[...] = _rms(_dot_nt(wcq_ref[...], h_ref[...]), gq_ref[...], axis=0).astype(BF16)

    zt = _dot(wuq_ref[...], cq_scr[...])
    tm = zt.shape[1]
    gh = gh_ref[...]
    g_n = jnp.broadcast_to(gh[:nope], (nope, tm))
    g_r = jnp.broadcast_to(gh[nope:nope + rope], (rope, tm))
    g_s = jnp.broadcast_to(gh[nope + rope:], (rope, tm))
    cos, sin = cos_ref[...], sin_ref[...]
    pad = jnp.zeros((LANES - rope, tm), F32)
    per = nope + LANES
    for hh in range(hg):
        base = hh * per
        qn = _rms(zt[base:base + nope, :], g_n, axis=0)
        r = zt[base + nope:base + nope + rope, :]
        inv = lax.rsqrt(jnp.mean(r * r, axis=0, keepdims=True) + NORM_EPS)
        rot = (r * inv * g_r) * cos + (zt[base + nope + rope:base + per, :] * inv * g_s) * sin
        o_ref[hh] = jnp.concatenate([qn * qscale, rot * qscale, pad], axis=0).astype(BF16)


def _mla_q(h, wcqt, gq, wuqt, gh, cos_t, sin_t, layer, *, heads, nope, rope, qscale, tm, hg):
    n, d = h.shape
    ql = wcqt.shape[1]
    per = nope + LANES
    return pl.pallas_call(
        functools.partial(_mla_q_kernel, hg=hg, nope=nope, rope=rope, qscale=qscale),
        grid=(n // tm, heads // hg),
        in_specs=[pl.BlockSpec((tm, d), lambda i, j: (i, 0)),
                  _layer_spec((ql, d), layer, lambda i, j: (0, 0)),
                  _layer_spec((ql, 1), layer, lambda i, j: (0, 0)),
                  _layer_spec((hg * per, ql), layer, lambda i, j: (j, 0)),
                  _layer_spec((per, 1), layer, lambda i, j: (0, 0)),
                  pl.BlockSpec((rope, tm), lambda i, j: (0, i)),
                  pl.BlockSpec((rope, tm), lambda i, j: (0, i))],
        out_specs=pl.BlockSpec((hg, per, tm), lambda i, j: (j, 0, i)),
        out_shape=jax.ShapeDtypeStruct((heads, per, n), BF16),
        scratch_shapes=[pltpu.VMEM((ql, tm), BF16)],
        compiler_params=_params("parallel", "arbitrary"),
        name="mla_q",
    )(h, wcqt, gq, wuqt, gh, cos_t, sin_t)


def _rotary(y, cos, sin):
    return y * cos + pltpu.roll(y, LANES // 2, 1) * sin


def _log_sigmoid(x):
    return jnp.minimum(x, 0.0) - jnp.log1p(jnp.exp(-jnp.abs(x)))


def _mla_kv_kernel(*refs, up, hg, kvl, rope, fh, nope, vdim, tk):
    it = iter(refs)
    h_ref, w6_ref, gkv_ref, gk_ref, b_ref, cos_ref, sin_ref = (next(it) for _ in range(7))
    if up:
        wuk_ref, wuvt_ref, gkn_ref = next(it), next(it), next(it)
    for _ in range(3):
        next(it)
    ckv_o, kpe_o, logf_o = next(it), next(it), next(it)
    if up:
        kcat_o, vt_o, ckv_scr, kpe_scr = (next(it) for _ in range(4))

    @pl.when(pl.program_id(1) == 0)
    def _():
        z = _dot(h_ref[...], w6_ref[...])
        ckv = _rms(z[:, :kvl], gkv_ref[...])
        ckv_o[...] = ckv
        rot = _rotary(_rms(z[:, kvl:kvl + LANES], gk_ref[...]), cos_ref[...], sin_ref[...])
        kpe_o[...] = rot[:, :rope]
        logf_o[...] = _log_sigmoid(z[:, kvl + LANES:] + b_ref[...])[:, :fh]
        if up:
            ckv_scr[...] = ckv.astype(BF16)
            kpe_scr[...] = rot.astype(BF16)

    if up:
        ckv = ckv_scr[...]
        kn = _dot(ckv, wuk_ref[...])
        vt = _dot_nt(wuvt_ref[...], ckv)
        for hh in range(hg):
            knh = _rms(kn[:, hh * nope:(hh + 1) * nope], gkn_ref[...]).astype(BF16)
            kcat_o[hh] = jnp.concatenate([knh, kpe_scr[...]], axis=1)
            for c in range(vt.shape[1] // tk):
                vt_o[hh, c] = vt[hh * vdim:(hh + 1) * vdim, c * tk:(c + 1) * tk].astype(BF16)


def _mla_kv(h, w6, gkv, gk, b, cos, sin, wuk, wuvt, gkn, states, layer, *, up, heads, nope, vdim, rope, fh, tm, tk, hg):
    n, d = h.shape
    kvl = gkv.shape[2]
    nj = heads // hg if up else 1
    c2 = lambda shape: _layer_spec(shape, layer, lambda i, j: (0,) * len(shape))
    ins = [h, w6, gkv, gk, b, cos, sin]
    in_specs = [pl.BlockSpec((tm, d), lambda i, j: (i, 0)), c2(w6.shape[1:]), c2((1, kvl)), c2((1, LANES)),
                c2((1, LANES)), pl.BlockSpec((tm, LANES), lambda i, j: (i, 0)),
                pl.BlockSpec((tm, LANES), lambda i, j: (i, 0))]
    if up:
        ins += [wuk, wuvt, gkn]
        in_specs += [_layer_spec((kvl, hg * nope), layer, lambda i, j: (0, j)),
                     _layer_spec((hg * vdim, kvl), layer, lambda i, j: (j, 0)), c2((1, nope))]
    aliases = {len(ins) + k: k for k in range(3)}
    ins += list(states)
    in_specs += [pl.BlockSpec(memory_space=pl.ANY)] * 3
    out_shape = [jax.ShapeDtypeStruct(s.shape, F32) for s in states]
    out_specs = [pl.BlockSpec((None, tm, w), lambda i, j: (layer, i, 0)) for w in (kvl, rope, fh)]
    scratch = []
    if up:
        out_shape += [jax.ShapeDtypeStruct((heads, n, nope + LANES), BF16),
                      jax.ShapeDtypeStruct((heads, n // tk, vdim, tk), BF16)]
        out_specs += [pl.BlockSpec((hg, tm, nope + LANES), lambda i, j: (j, i, 0)),
                      pl.BlockSpec((hg, tm // tk, vdim, tk), lambda i, j: (j, i, 0, 0))]
        scratch = [pltpu.VMEM((tm, kvl), BF16), pltpu.VMEM((tm, LANES), BF16)]
    return pl.pallas_call(
        functools.partial(_mla_kv_kernel, up=up, hg=hg, kvl=kvl, rope=rope, fh=fh, nope=nope, vdim=vdim, tk=tk),
        grid=(n // tm, nj),
        in_specs=in_specs,
        out_specs=out_specs,
        out_shape=out_shape,
        input_output_aliases=aliases,
        scratch_shapes=scratch,
        compiler_params=_params("parallel", "arbitrary"),
        name="mla_kv",
    )(*ins)


def _cumsum_kernel(x_ref, tri_ref, o_ref):
    tri = tri_ref[...]
    carry = jnp.zeros((x_ref.shape[0], 1), F32)
    for c in range(x_ref.shape[1] // CUMSUM_BLOCK):
        sl = slice(c * CUMSUM_BLOCK, (c + 1) * CUMSUM_BLOCK)
        hi, mid, lo = _split3(x_ref[:, sl])
        blk = _dot(hi, tri) + _dot(mid, tri) + _dot(lo, tri) + carry
        o_ref[:, sl] = blk
        carry = blk[:, CUMSUM_BLOCK - 1:]


def _cumsum_time(x):
    b, hh, s = x.shape
    r = lax.broadcasted_iota(jnp.int32, (CUMSUM_BLOCK, CUMSUM_BLOCK), 0)
    c = lax.broadcasted_iota(jnp.int32, (CUMSUM_BLOCK, CUMSUM_BLOCK), 1)
    tri = (r <= c).astype(BF16)
    return pl.pallas_call(
        _cumsum_kernel,
        grid=(b,),
        in_specs=[pl.BlockSpec((None, hh, s), lambda i: (i, 0, 0)),
                  pl.BlockSpec((CUMSUM_BLOCK, CUMSUM_BLOCK), lambda i: (0, 0))],
        out_specs=pl.BlockSpec((None, hh, s), lambda i: (i, 0, 0)),
        out_shape=jax.ShapeDtypeStruct(x.shape, F32),
        compiler_params=_params("parallel"),
        name="cumsum_logf",
    )(x, tri)


def _attn_prompt_kernel(qt_ref, k_ref, vt_ref, o_ref, sa_scr, mxa_scr, sb_scr, mxb_scr, m_scr, l_scr, acc_scr,
                        *, hg, tq, chunk):
    i = pl.program_id(2)
    m_scr[...] = jnp.full_like(m_scr, MASKED)
    l_scr[...] = jnp.zeros_like(l_scr)
    acc_scr[...] = jnp.zeros_like(acc_scr)
    bufs = {"a": (sa_scr, mxa_scr), "b": (sb_scr, mxb_scr)}

    def stash(j, buf, diagonal=False):
        s_ref, mx_ref = bufs[buf]
        start = pl.multiple_of(j * tq, tq)
        for hh in range(hg):
            t = _dot(k_ref[hh, pl.ds(start, tq), :], qt_ref[hh])
            if diagonal:
                ks = lax.broadcasted_iota(jnp.int32, t.shape, 0)
                qs = lax.broadcasted_iota(jnp.int32, t.shape, 1)
                vis = (ks // chunk) <= (qs // chunk) if chunk > 1 else ks <= qs
                t = jnp.where(vis, t, MASKED)
            s_ref[hh] = t
            mx_ref[hh] = jnp.max(t, axis=0, keepdims=True)

    def absorb(j, buf):
        s_ref, mx_ref = bufs[buf]
        for hh in range(hg):
            m_old = m_scr[hh]
            m_new = jnp.maximum(m_old, mx_ref[hh])
            alpha = jnp.exp2(m_old - m_new)
            p = jnp.exp2(s_ref[hh] - m_new)
            l_scr[hh] = alpha * l_scr[hh] + jnp.sum(p, axis=0, keepdims=True)
            acc_scr[hh] = alpha * acc_scr[hh] + _dot(vt_ref[hh, j], p.astype(BF16))
            m_scr[hh] = m_new

    lead = (i + 1) % 2

    @pl.when((i >= 1) & (lead == 0))
    def _():
        stash(0, "a")

    @pl.when((i >= 1) & (lead == 1))
    def _():
        stash(0, "b")
        stash(1, "a")
        absorb(0, "b")

    def body(jj, carry):
        t0 = lead + 2 * jj
        stash(t0 + 1, "b")
        absorb(t0, "a")
        stash(t0 + 2, "a")
        absorb(t0 + 1, "b")
        return carry

    lax.fori_loop(0, (i - 1 - lead) // 2, body, 0)

    @pl.when(i >= 1)
    def _():
        stash(i, "b", diagonal=True)
        absorb(i - 1, "a")

    @pl.when(i == 0)
    def _():
        stash(i, "b", diagonal=True)

    absorb(i, "b")
    o_ref[...] = jnp.concatenate([(acc_scr[hh] / l_scr[hh]).T for hh in range(hg)], axis=1).astype(o_ref.dtype)


def _attn_prompt(qt, k, vt, *, batch, seq, tq, chunk, hg):
    heads, dk, n = qt.shape
    dv = vt.shape[2]
    nq = seq // tq
    return pl.pallas_call(
        functools.partial(_attn_prompt_kernel, hg=hg, tq=tq, chunk=chunk),
        grid=(batch, heads // hg, nq),
        in_specs=[pl.BlockSpec((hg, dk, tq), lambda b, g, i: (g, 0, b * nq + i)),
                  pl.BlockSpec((hg, seq, dk), lambda b, g, i: (g, b, 0)),
                  pl.BlockSpec((hg, nq, dv, tq), lambda b, g, i: (g, b, 0, 0))],
        out_specs=pl.BlockSpec((tq, hg * dv), lambda b, g, i: (b * nq + i, g)),
        out_shape=jax.ShapeDtypeStruct((n, heads * dv), BF16),
        scratch_shapes=[pltpu.VMEM((hg, tq, tq), F32), pltpu.VMEM((hg, 1, tq), F32),
                        pltpu.VMEM((hg, tq, tq), F32), pltpu.VMEM((hg, 1, tq), F32),
                        pltpu.VMEM((hg, 1, tq), F32), pltpu.VMEM((hg, 1, tq), F32), pltpu.VMEM((hg, dv, tq), F32)],
        compiler_params=_params("parallel", "parallel", "arbitrary"),
        name="attn_prompt_chunk%d" % chunk,
    )(qt, k, vt)


def _softmax_two_segments(s_c, s_n, v_c, v_n, transposed_v=False):
    m = jnp.maximum(jnp.max(s_c, axis=1, keepdims=True), jnp.max(s_n, axis=1, keepdims=True))
    p_c = jnp.exp(s_c - m)
    p_n = jnp.exp(s_n - m)
    l = jnp.sum(p_c, axis=1, keepdims=True) + jnp.sum(p_n, axis=1, keepdims=True)
    pv = _dot_nt if transposed_v else _dot
    return (pv(p_c.astype(BF16), v_c) + pv(p_n.astype(BF16), v_n)) / l


def _fox_sample_kernel(q_ref, kc_ref, vc_ref, kn_ref, vn_ref, fq_ref, fkc_ref, fkn_ref, o_ref, *, heads, hd, scale):
    t = q_ref.shape[0]
    rows = lax.broadcasted_iota(jnp.int32, (t, t), 0)
    cols = lax.broadcasted_iota(jnp.int32, (t, t), 1)
    outs = []
    for hh in range(heads):
        sl = slice(hh * hd, (hh + 1) * hd)
        q = q_ref[:, sl]
        fq = fq_ref[hh]
        rows_h = pl.ds(hh, kc_ref.shape[0] // heads, stride=heads)
        s_c = _dot_nt(q, kc_ref[rows_h, :].astype(BF16)) * scale + (fq - fkc_ref[hh])
        s_n = _dot_nt(q, kn_ref[:, sl].astype(BF16)) * scale + (fq - fkn_ref[hh])
        s_n = jnp.where(cols <= rows, s_n, MASKED)
        outs.append(_softmax_two_segments(s_c, s_n, vc_ref[rows_h, :].astype(BF16), vn_ref[:, sl].astype(BF16)))
    o_ref[...] = jnp.concatenate(outs, axis=1).astype(o_ref.dtype)


def _fox_sample(q, kc, vc, k_state, v_state, layer, fq, fkc, fkn):
    n, width = q.shape
    depth, b, s, heads, hd = kc.shape
    t = n // b
    kc, vc = (a.reshape(depth, b, s * heads, hd) for a in (kc, vc))
    cache = pl.BlockSpec((None, None, s * heads, hd), lambda i: (layer, i, 0, 0))
    new = pl.BlockSpec((None, t, width), lambda i: (layer, i, 0))
    return pl.pallas_call(
        functools.partial(_fox_sample_kernel, heads=heads, hd=hd, scale=float(hd) ** -0.5),
        grid=(b,),
        in_specs=[pl.BlockSpec((t, width), lambda i: (i, 0)), cache, cache, new, new,
                  pl.BlockSpec((None, heads, t, 1), lambda i: (i, 0, 0, 0)),
                  pl.BlockSpec((None, heads, 1, s), lambda i: (i, 0, 0, 0)),
                  pl.BlockSpec((None, heads, 1, t), lambda i: (i, 0, 0, 0))],
        out_specs=pl.BlockSpec((t, width), lambda i: (i, 0)),
        out_shape=jax.ShapeDtypeStruct((n, width), BF16),
        compiler_params=_params("parallel"),
        name="fox_sample",
    )(q, kc, vc, k_state, v_state, fq, fkc, fkn)


def _mla_sample_kernel(q_ref, cc_ref, pc_ref, cn_ref, pn_ref, wkt_ref, wuv_ref, gkn_ref, o_ref, cc_scr, pct_scr,
                       *, hg, nope, vdim, past, scale, chunk):
    t = q_ref.shape[1]
    rows = lax.broadcasted_iota(jnp.int32, (t, t), 0) + past
    cols = lax.broadcasted_iota(jnp.int32, (t, t), 1) + past
    vis = (cols // chunk) <= (rows // chunk)

    def rope_rows(p_ref):
        pe = p_ref[...]
        pad = jnp.zeros((pe.shape[0], LANES - pe.shape[1]), F32)
        return jnp.concatenate([pe, pad], axis=1).T.astype(BF16)

    @pl.when(pl.program_id(1) == 0)
    def _():
        cc_scr[...] = cc_ref[...].astype(BF16)
        pct_scr[...] = rope_rows(pc_ref)

    pnt = rope_rows(pn_ref)
    cn = cn_ref[...].astype(BF16)
    knt_c = _dot_nt(wkt_ref[...], cc_scr[...])
    knt_n = _dot_nt(wkt_ref[...], cn)
    g = gkn_ref[...]
    p_cs, p_ns, ls = [], [], []
    for hh in range(hg):
        q = q_ref[hh]

        def keys_t(knt, pet):
            kn = _rms(knt[hh * nope:(hh + 1) * nope, :], g, axis=0).astype(BF16)
            return jnp.concatenate([kn, pet], axis=0)

        s_c = _dot(q, keys_t(knt_c, pct_scr[...])) * scale
        s_n = jnp.where(vis, _dot(q, keys_t(knt_n, pnt)) * scale, MASKED)
        m = jnp.maximum(jnp.max(s_c, axis=1, keepdims=True), jnp.max(s_n, axis=1, keepdims=True))
        p_c = jnp.exp(s_c - m)
        p_n = jnp.exp(s_n - m)
        ls.append(jnp.sum(p_c, axis=1, keepdims=True) + jnp.sum(p_n, axis=1, keepdims=True))
        p_cs.append(p_c.astype(BF16))
        p_ns.append(p_n.astype(BF16))
    ctx = _dot(jnp.concatenate(p_cs, axis=0), cc_scr[...]) + _dot(jnp.concatenate(p_ns, axis=0), cn)
    outs = [_dot(ctx[hh * t:(hh + 1) * t, :].astype(BF16), wuv_ref[:, hh * vdim:(hh + 1) * vdim]) / ls[hh]
            for hh in range(hg)]
    o_ref[...] = jnp.concatenate(outs, axis=1).astype(o_ref.dtype)


def _mla_sample(q, ckv_cache, kpe_cache, ckv_state, kpe_state, layer, wukt, wuv, gkn, *, heads, nope, vdim, hg, chunk):
    _, n, dk = q.shape
    _, b, s, kvl = ckv_cache.shape
    rope = kpe_cache.shape[3]
    t = n // b
    return pl.pallas_call(
        functools.partial(_mla_sample_kernel, hg=hg, nope=nope, vdim=vdim, past=s,
                          scale=float(nope + rope) ** -0.5, chunk=chunk),
        grid=(b, heads // hg),
        in_specs=[pl.BlockSpec((hg, t, dk), lambda i, g: (g, i, 0)),
                  pl.BlockSpec((None, None, s, kvl), lambda i, g: (layer, i, 0, 0)),
                  pl.BlockSpec((None, None, s, rope), lambda i, g: (layer, i, 0, 0)),
                  pl.BlockSpec((None, t, kvl), lambda i, g: (layer, i, 0)),
                  pl.BlockSpec((None, t, rope), lambda i, g: (layer, i, 0)),
                  _layer_spec((hg * nope, kvl), layer, lambda i, g: (g, 0)),
                  _layer_spec((kvl, hg * vdim), layer, lambda i, g: (0, g)),
                  _layer_spec((nope, 1), layer, lambda i, g: (0, 0))],
        out_specs=pl.BlockSpec((t, hg * vdim), lambda i, g: (i, g)),
        out_shape=jax.ShapeDtypeStruct((n, heads * vdim), BF16),
        scratch_shapes=[pltpu.VMEM((s, kvl), BF16), pltpu.VMEM((LANES, s), BF16)],
        compiler_params=_params("parallel", "arbitrary"),
        name="mla_sample",
    )(q, ckv_cache, kpe_cache, ckv_state, kpe_state, wukt, wuv, gkn)


def _merge_kernel(x_ref, h_ref, of_ref, om_ref, wga_ref, wgb_ref, wa_ref, wb_ref, wo_ref, o_ref):
    j = pl.program_id(1)

    @pl.when(j == 0)
    def _():
        o_ref[...] = jnp.zeros_like(o_ref)

    h = h_ref[...]
    ga = jax.nn.sigmoid(_dot(h, wga_ref[...]))
    gb = jax.nn.sigmoid(_dot(h, wgb_ref[...]))
    m = ga * _dot(of_ref[...], wa_ref[...]) + gb * _dot(om_ref[...], wb_ref[...])
    o_ref[...] += _dot(m.astype(BF16), wo_ref[...])

    @pl.when(j == pl.num_programs(1) - 1)
    def _():
        o_ref[...] = x_ref[...] + o_ref[...]


def _merge(x, h, of, om, wga, wgb, wa, wb, wo, layer, *, tm, tn):
    n, d = x.shape
    row = lambda c: pl.BlockSpec((tm, c), lambda i, j: (i, 0))
    col = lambda r: _layer_spec((r, tn), layer, lambda i, j: (0, j))
    return pl.pallas_call(
        _merge_kernel,
        grid=(n // tm, d // tn),
        in_specs=[row(d), row(d), row(of.shape[1]), row(om.shape[1]),
                  col(d), col(d), col(of.shape[1]), col(om.shape[1]),
                  _layer_spec((tn, d), layer, lambda i, j: (j, 0))],
        out_specs=row(d),
        out_shape=jax.ShapeDtypeStruct((n, d), F32),
        compiler_params=_params("parallel", "arbitrary"),
        name="merge_out",
    )(x, h, of, om, wga, wgb, wa, wb, wo)


def _swap_halves(a):
    half = a.shape[-1] // 2
    return jnp.concatenate([a[..., half:], a[..., :half]], axis=-1)


def _rope_angles(pos, rope):
    half = rope // 2
    inv = ROPE_THETA ** (-jnp.arange(half, dtype=F32) / half)
    ang = pos.astype(F32)[:, None] * inv[None, :]
    return jnp.cos(ang), jnp.sin(ang)


def kernel(x_prompt, x_sample, cache_fox_k, cache_fox_v, cache_fox_logf, cache_mla_ckv, cache_mla_kpe, ffn1_norm, ffn1_w_gate, ffn1_w_up, ffn1_w_down, mix_norm, w_in, b_forget, fox_q_norm, fox_k_norm, mla_q_lat_norm, w_uq, mla_q_nope_norm, mla_q_rope_norm, mla_kv_lat_norm, w_ukv, mla_k_nope_norm, mla_k_rope_norm, w_branch_a, w_branch_b, w_out, ffn2_norm, ffn2_w_gate, ffn2_w_up, ffn2_w_down, post_norm):
    depth, d = ffn1_norm.shape
    bp, tp, _ = x_prompt.shape
    bs, ts, _ = x_sample.shape
    past, fh, hd = cache_fox_k.shape[2], cache_fox_k.shape[3], cache_fox_k.shape[4]
    fw = fh * hd
    ql, kvl = mla_q_lat_norm.shape[1], mla_kv_lat_norm.shape[1]
    nope, rope = mla_q_nope_norm.shape[1], mla_q_rope_norm.shape[1]
    mh = w_uq.shape[2] // (nope + rope)
    vdim = w_ukv.shape[2] // mh - nope
    assert nope == LANES and 2 * rope == LANES and hd == LANES and vdim == LANES
    np_, ns = bp * tp, bs * ts

    bf = lambda a: a.astype(BF16)
    tr = lambda a: jnp.swapaxes(a, 1, 2)
    o = [0]
    for wdt in (fw, fw, fw, fh, ql, kvl, rope, d, d):
        o.append(o[-1] + wdt)
    w_qt = bf(tr(w_in[:, :, o[0]:o[1]]))
    w_k, w_v = bf(w_in[:, :, o[1]:o[2]]), bf(w_in[:, :, o[2]:o[3]])
    w_f, w_cq, w_ckv, w_kpe = (w_in[:, :, o[k]:o[k + 1]] for k in range(3, 7))
    w_ga, w_gb = bf(w_in[:, :, o[7]:o[8]]), bf(w_in[:, :, o[8]:o[9]])
    w_cqt = bf(tr(w_cq))
    w6 = bf(jnp.concatenate([w_ckv, w_kpe, _swap_halves(w_kpe), w_f,
                             jnp.zeros((depth, d, LANES - fh), F32)], axis=2))
    b6 = jnp.concatenate([b_forget, jnp.zeros((depth, LANES - fh), F32)], axis=1)
    uq = w_uq.reshape(depth, ql, mh, nope + rope)
    wuqt = bf(tr(jnp.concatenate([uq, _swap_halves(uq[..., nope:])], axis=3).reshape(depth, ql, mh * (nope + LANES))))
    g_qh = jnp.concatenate([mla_q_nope_norm, mla_q_rope_norm, _swap_halves(mla_q_rope_norm)], axis=1)
    g_kr = jnp.concatenate([mla_k_rope_norm, _swap_halves(mla_k_rope_norm)], axis=1)
    ukv = w_ukv.reshape(depth, kvl, mh, nope + vdim)
    wuk = bf(ukv[..., :nope].reshape(depth, kvl, mh * nope))
    wuv = bf(ukv[..., nope:].reshape(depth, kvl, mh * vdim))
    wukt, wuvt = tr(wuk), tr(wuv)
    f1g, f1u, f1d = bf(ffn1_w_gate), bf(ffn1_w_up), bf(ffn1_w_down)
    f2g, f2u, f2d = bf(ffn2_w_gate), bf(ffn2_w_up), bf(ffn2_w_down)
    wa, wb, wo = bf(w_branch_a), bf(w_branch_b), bf(w_out)
    rowv = lambda a: a[:, None, :]
    colv = lambda a: a[:, :, None]

    def rope_tables(pos, reps):
        cos, sin = _rope_angles(pos, rope)
        pad = jnp.zeros((pos.shape[0], LANES - rope), F32)
        lanes = (jnp.concatenate([cos, cos, pad], axis=1), jnp.concatenate([-sin, sin, pad], axis=1))
        subl = (jnp.concatenate([cos, cos], axis=1).T, jnp.concatenate([-sin, sin], axis=1).T)
        return [jnp.tile(a, (reps, 1)) for a in lanes] + [jnp.tile(a, (1, reps)) for a in subl]

    rope_p = rope_tables(jnp.arange(tp, dtype=jnp.int32), bp)
    rope_s = rope_tables(past + jnp.arange(ts, dtype=jnp.int32), bs)

    def states(n):
        return [jnp.zeros((depth, n, w), F32) for w in (fw, fw, fh, kvl, rope)]

    st_p, st_s = states(np_), states(ns)

    tq = _tile(tp, ATTN_TILE)
    tm_p, tm_s = _tile(np_, 512), _tile(ns, 512)
    assert tm_p % tq == 0 and tp % tq == 0
    tf = _tile(f1g.shape[2], 512)
    tn = _tile(d, 512)
    s_pad = -(-(past + ts) // CUMSUM_BLOCK) * CUMSUM_BLOCK
    fox_scale, mla_scale = float(hd) ** -0.5, float(nope + rope) ** -0.5

    def tokens(x, st, l, *, prompt):
        tm = tm_p if prompt else tm_s
        cos, sin, cos_t, sin_t = rope_p if prompt else rope_s
        n = x.shape[0]
        x1, h = _ffn(x, rowv(ffn1_norm), f1g, f1u, f1d, rowv(mix_norm), l, post=False, tm=tm, tf=tf)
        res = _mla_kv(h, w6, rowv(mla_kv_lat_norm), rowv(g_kr), rowv(b6), cos, sin, wuk, wuvt,
                      rowv(mla_k_nope_norm), (st[3], st[4], st[2]), l, up=prompt, heads=mh, nope=nope,
                      vdim=vdim, rope=rope, fh=fh, tm=tm, tk=tq, hg=4)
        st[3], st[4], st[2] = res[:3]
        q_m = _mla_q(h, w_cqt, colv(mla_q_lat_norm), wuqt, colv(g_qh), cos_t, sin_t, l, heads=mh, nope=nope,
                     rope=rope, qscale=mla_scale * LOG2E if prompt else 1.0, tm=tm, hg=4)
        kv = functools.partial(_fox_kv, heads=fh, hd=hd, prompt=prompt, tm=tm, tk=tq)
        if prompt:
            logf_t = jnp.swapaxes(st[2][l].reshape(bp, tp, fh), 1, 2)
            fcum_t = _cumsum_time(logf_t)
            fcum = jnp.swapaxes(fcum_t, 1, 2).reshape(n, fh)
            fcum_t = jnp.swapaxes(fcum_t, 0, 1).reshape(fh, n)
            q_f = _fox_q(h, w_qt, colv(fox_q_norm), fcum_t, l, heads=fh, hd=hd, prompt=True, tm=tm)
            st[0], k_f = kv(h, w_k, rowv(fox_k_norm), fcum, st[0], l, mode="k")
            st[1], vt_f = kv(h, w_v, None, None, st[1], l, mode="v")
            o_f = _attn_prompt(q_f, k_f, vt_f, batch=bp, seq=tp, tq=tq, chunk=1, hg=ATTN_HEADS)
            o_m = _attn_prompt(q_m, res[3], res[4], batch=bp, seq=tp, tq=tq, chunk=CHUNK, hg=ATTN_HEADS)
        else:
            q_f = _fox_q(h, w_qt, colv(fox_q_norm), None, l, heads=fh, hd=hd, prompt=False, tm=tm)
            q_f = jnp.transpose(q_f, (2, 0, 1)).reshape(n, fw)
            (st[0],) = kv(h, w_k, rowv(fox_k_norm), None, st[0], l, mode="k")
            (st[1],) = kv(h, w_v, None, None, st[1], l, mode="v")
            logf_all = jnp.concatenate([cache_fox_logf[l], st[2][l].reshape(bs, ts, fh),
                                        jnp.zeros((bs, s_pad - past - ts, fh), F32)], axis=1)
            fall = _cumsum_time(jnp.swapaxes(logf_all, 1, 2))
            fkc = fall[:, :, None, :past]
            fkn = fall[:, :, None, past:past + ts]
            fq = fall[:, :, past:past + ts, None]
            o_f = _fox_sample(q_f, cache_fox_k, cache_fox_v, st[0], st[1], l, fq, fkc, fkn)
            o_m = _mla_sample(jnp.swapaxes(q_m, 1, 2), cache_mla_ckv, cache_mla_kpe, st[3], st[4], l, wukt, wuv,
                              colv(mla_k_nope_norm), heads=mh, nope=nope, vdim=vdim, hg=min(8, mh), chunk=CHUNK)
        x2 = _merge(x1, h, o_f, o_m, w_ga, w_gb, wa, wb, wo, l, tm=tm, tn=tn)
        return _ffn(x2, rowv(ffn2_norm), f2g, f2u, f2d, rowv(post_norm), l, post=True, tm=tm, tf=tf)

    y_p, y_s = x_prompt.reshape(np_, d), x_sample.reshape(ns, d)
    for l in range(depth):
        y_p = tokens(y_p, st_p, l, prompt=True)
        y_s = tokens(y_s, st_s, l, prompt=False)

    def shaped(st, b, t):
        return (st[0].reshape(depth, b, t, fh, hd), st[1].reshape(depth, b, t, fh, hd), st[2].reshape(depth, b, t, fh),
                st[3].reshape(depth, b, t, kvl), st[4].reshape(depth, b, t, rope))

    return (y_p.reshape(bp, tp, d), y_s.reshape(bs, ts, d)) + shaped(st_p, bp, tp) + shaped(st_s, bs, ts)
```

```python
import functools
import math

import jax
import jax.numpy as jnp
from jax import lax
from jax.experimental import pallas as pl
from jax.experimental.pallas import tpu as pltpu

F32 = jnp.float32
BF16 = jnp.bfloat16

NORM_EPS = 1e-6
CHUNK = 64
ROPE_THETA = 10000.0
MASKED = -1e30
LOG2E = math.log2(math.e)

LANES = 128
MXU_DIM = 256
VMEM_LIMIT_BYTES = 56 * 1024 * 1024
CUMSUM_BLOCK = MXU_DIM
ATTN_TILE = 512
ATTN_HEADS = 4


def _params(*semantics):
    return pltpu.CompilerParams(dimension_semantics=semantics, vmem_limit_bytes=VMEM_LIMIT_BYTES)


def _tile(n, pref):
    t = min(n, pref)
    while n % t:
        t -= 8
    return t


def _rms(x, g, axis=-1):
    return x * lax.rsqrt(jnp.mean(x * x, axis=axis, keepdims=True) + NORM_EPS) * g


def _dot(a, b):
    return jnp.dot(a, b, preferred_element_type=F32)


def _dot_nt(a, b):
    return lax.dot_general(a, b, (((1,), (1,)), ((), ())), preferred_element_type=F32)


def _split3(x):
    hi = x.astype(BF16)
    r = x - hi.astype(F32)
    mid = r.astype(BF16)
    lo = (r - mid.astype(F32)).astype(BF16)
    return hi, mid, lo


def _layer_spec(shape, layer, index_map):
    return pl.BlockSpec((None,) + tuple(shape), lambda *ids: (layer,) + tuple(index_map(*ids)))


def _ffn_kernel(x_ref, g_ref, wg_ref, wu_ref, wd_ref, g2_ref, o_ref, *rest, post):
    if post:
        (h_scr,) = rest
    else:
        h2_ref, h_scr = rest
    j = pl.program_id(1)

    @pl.when(j == 0)
    def _():
        h_scr[...] = _rms(x_ref[...], g_ref[...]).astype(BF16)
        o_ref[...] = jnp.zeros_like(o_ref)

    h = h_scr[...]
    g = _dot(h, wg_ref[...])
    u = _dot(h, wu_ref[...])
    a = (g * jax.nn.sigmoid(g) * u).astype(BF16)
    o_ref[...] += _dot(a, wd_ref[...])

    @pl.when(j == pl.num_programs(1) - 1)
    def _():
        y = x_ref[...] + 0.5 * o_ref[...]
        if post:
            o_ref[...] = _rms(y, g2_ref[...])
        else:
            o_ref[...] = y
            h2_ref[...] = _rms(y, g2_ref[...]).astype(BF16)


def _ffn(x, g, wg, wu, wd, g2, layer, *, post, tm, tf):
    n, d = x.shape
    dff = wg.shape[2]
    row = pl.BlockSpec((tm, d), lambda i, j: (i, 0))
    vec = _layer_spec((1, d), layer, lambda i, j: (0, 0))
    out_shape = [jax.ShapeDtypeStruct((n, d), F32)]
    out_specs = [row]
    if not post:
        out_shape.append(jax.ShapeDtypeStruct((n, d), BF16))
        out_specs.append(row)
    res = pl.pallas_call(
        functools.partial(_ffn_kernel, post=post),
        grid=(n // tm, dff // tf),
        in_specs=[row, vec,
                  _layer_spec((d, tf), layer, lambda i, j: (0, j)),
                  _layer_spec((d, tf), layer, lambda i, j: (0, j)),
                  _layer_spec((tf, d), layer, lambda i, j: (j, 0)),
                  vec],
        out_specs=out_specs,
        out_shape=out_shape,
        scratch_shapes=[pltpu.VMEM((tm, d), BF16)],
        compiler_params=_params("parallel", "arbitrary"),
        name="ffn_post" if post else "ffn",
    )(x, g, wg, wu, wd, g2)
    return res[0] if post else res


def _bias_slots(f, axis, *, query):
    hi, mid, lo = (p.astype(F32) for p in _split3(f))
    shape = (LANES, f.shape[1]) if axis == 0 else (f.shape[0], LANES)
    slot = lax.broadcasted_iota(jnp.int32, shape, axis)
    if not query:
        hi, mid, lo = -hi, -mid, -lo
        slot = slot - 3
    vals = jnp.where(slot == 0, hi, jnp.where(slot == 1, mid, jnp.where(slot == 2, lo, 0.0)))
    ones = (slot >= 3) & (slot < 6) if query else (slot >= -3) & (slot < 0)
    return jnp.where(ones, 1.0, vals)


def _fox_q_kernel(*refs, heads, hd, with_bias, qscale):
    if with_bias:
        h_ref, w_ref, g_ref, f_ref, o_ref = refs
    else:
        h_ref, w_ref, g_ref, o_ref = refs
    zt = _dot_nt(w_ref[...], h_ref[...])
    g = jnp.broadcast_to(g_ref[...], (hd, zt.shape[1]))
    for hh in range(heads):
        qn = _rms(zt[hh * hd:(hh + 1) * hd, :], g, axis=0) * qscale
        if with_bias:
            bias = _bias_slots(f_ref[hh:hh + 1, :] * LOG2E, 0, query=True)
            o_ref[hh] = jnp.concatenate([qn, bias], axis=0).astype(BF16)
        else:
            o_ref[hh] = qn.astype(BF16)


def _fox_q(h, wqt, g, fcum_t, layer, *, heads, hd, prompt, tm):
    n, d = h.shape
    ins = [h, wqt, g]
    in_specs = [pl.BlockSpec((tm, d), lambda i: (i, 0)),
                _layer_spec((heads * hd, d), layer, lambda i: (0, 0)),
                _layer_spec((hd, 1), layer, lambda i: (0, 0))]
    if prompt:
        ins.append(fcum_t)
        in_specs.append(pl.BlockSpec((heads, tm), lambda i: (0, i)))
    dk = hd + LANES if prompt else hd
    return pl.pallas_call(
        functools.partial(_fox_q_kernel, heads=heads, hd=hd, with_bias=prompt,
                          qscale=float(hd) ** -0.5 * LOG2E if prompt else 1.0),
        grid=(n // tm,),
        in_specs=in_specs,
        out_specs=pl.BlockSpec((heads, dk, tm), lambda i: (0, 0, i)),
        out_shape=jax.ShapeDtypeStruct((heads, dk, n), BF16),
        compiler_params=_params("parallel"),
        name="fox_q",
    )(*ins)


def _fox_kv_kernel(*refs, mode, heads, hd, with_bias, tk):
    it = iter(refs)
    h_ref, w_ref = next(it), next(it)
    g_ref = next(it) if mode == "k" else None
    f_ref = next(it) if with_bias else None
    next(it)
    state_ref = next(it)
    o_ref = next(it, None)
    z = _dot(h_ref[...], w_ref[...])
    if with_bias:
        fs = f_ref[...] * LOG2E
    outs = []
    for hh in range(heads):
        zh = z[:, hh * hd:(hh + 1) * hd]
        if mode == "k":
            zh = _rms(zh, g_ref[...])
        outs.append(zh)
        if o_ref is None:
            continue
        if mode == "v":
            for c in range(zh.shape[0] // tk):
                o_ref[hh, c] = zh[c * tk:(c + 1) * tk, :].T.astype(BF16)
        else:
            o_ref[hh] = jnp.concatenate([zh, _bias_slots(fs[:, hh:hh + 1], 1, query=False)], axis=1).astype(BF16)
    state_ref[...] = jnp.concatenate(outs, axis=1)


def _fox_kv(h, w, g, fcum, state, layer, *, mode, heads, hd, prompt, tm, tk):
    n, d = h.shape
    width = heads * hd
    with_bias = prompt and mode == "k"
    row = lambda c: pl.BlockSpec((tm, c), lambda i: (i, 0))
    ins, in_specs = [h, w], [row(d), _layer_spec((d, width), layer, lambda i: (0, 0))]
    if mode == "k":
        ins.append(g)
        in_specs.append(_layer_spec((1, hd), layer, lambda i: (0, 0)))
    if with_bias:
        ins.append(fcum)
        in_specs.append(row(heads))
    aliases = {len(ins): 0}
    ins.append(state)
    in_specs.append(pl.BlockSpec(memory_space=pl.ANY))
    out_shape = [jax.ShapeDtypeStruct(state.shape, F32)]
    out_specs = [pl.BlockSpec((None, tm, width), lambda i: (layer, i, 0))]
    if prompt and mode == "v":
        out_shape.append(jax.ShapeDtypeStruct((heads, n // tk, hd, tk), BF16))
        out_specs.append(pl.BlockSpec((heads, tm // tk, hd, tk), lambda i: (0, i, 0, 0)))
    elif with_bias:
        out_shape.append(jax.ShapeDtypeStruct((heads, n, hd + LANES), BF16))
        out_specs.append(pl.BlockSpec((heads, tm, hd + LANES), lambda i: (0, i, 0)))
    return pl.pallas_call(
        functools.partial(_fox_kv_kernel, mode=mode, heads=heads, hd=hd, with_bias=with_bias, tk=tk),
        grid=(n // tm,),
        in_specs=in_specs,
        out_specs=out_specs,
        out_shape=out_shape,
        input_output_aliases=aliases,
        compiler_params=_params("parallel"),
        name="fox_" + mode,
    )(*ins)


def _mla_q_kernel(h_ref, wcq_ref, gq_ref, wuq_ref, gh_ref, cos_ref, sin_ref, o_ref, cq_scr, *, hg, nope, rope, qscale):
    @pl.when(pl.program_id(1) == 0)
    def _():
        cq_scr[...] = _rms(_dot_nt(wcq_ref[...], h_ref[...]), gq_ref[...], axis=0).astype(BF16)

    zt = _dot(wuq_ref[...], cq_scr[...])
    tm = zt.shape[1]
    gh = gh_ref[...]
    g_n = jnp.broadcast_to(gh[:nope], (nope, tm))
    g_r = jnp.broadcast_to(gh[nope:nope + rope], (rope, tm))
    g_s = jnp.broadcast_to(gh[nope + rope:], (rope, tm))
    cos, sin = cos_ref[...], sin_ref[...]
    pad = jnp.zeros((LANES - rope, tm), F32)
    per = nope + LANES
    for hh in range(hg):
        base = hh * per
        qn = _rms(zt[base:base + nope, :], g_n, axis=0)
        r = zt[base + nope:base + nope + rope, :]
        inv = lax.rsqrt(jnp.mean(r * r, axis=0, keepdims=True) + NORM_EPS)
        rot = (r * inv * g_r) * cos + (zt[base + nope + rope:base + per, :] * inv * g_s) * sin
        o_ref[hh] = jnp.concatenate([qn * qscale, rot * qscale, pad], axis=0).astype(BF16)


def _mla_q(h, wcqt, gq, wuqt, gh, cos_t, sin_t, layer, *, heads, nope, rope, qscale, tm, hg):
    n, d = h.shape
    ql = wcqt.shape[1]
    per = nope + LANES
    return pl.pallas_call(
        functools.partial(_mla_q_kernel, hg=hg, nope=nope, rope=rope, qscale=qscale),
        grid=(n // tm, heads // hg),
        in_specs=[pl.BlockSpec((tm, d), lambda i, j: (i, 0)),
                  _layer_spec((ql, d), layer, lambda i, j: (0, 0)),
                  _layer_spec((ql, 1), layer, lambda i, j: (0, 0)),
                  _layer_spec((hg * per, ql), layer, lambda i, j: (j, 0)),
                  _layer_spec((per, 1), layer, lambda i, j: (0, 0)),
                  pl.BlockSpec((rope, tm), lambda i, j: (0, i)),
                  pl.BlockSpec((rope, tm), lambda i, j: (0, i))],
        out_specs=pl.BlockSpec((hg, per, tm), lambda i, j: (j, 0, i)),
        out_shape=jax.ShapeDtypeStruct((heads, per, n), BF16),
        scratch_shapes=[pltpu.VMEM((ql, tm), BF16)],
        compiler_params=_params("parallel", "arbitrary"),
        name="mla_q",
    )(h, wcqt, gq, wuqt, gh, cos_t, sin_t)


def _rotary(y, cos, sin):
    return y * cos + pltpu.roll(y, LANES // 2, 1) * sin


def _log_sigmoid(x):
    return jnp.minimum(x, 0.0) - jnp.log1p(jnp.exp(-jnp.abs(x)))


def _mla_kv_kernel(*refs, up, hg, kvl, rope, fh, nope, vdim, tk):
    it = iter(refs)
    h_ref, w6_ref, gkv_ref, gk_ref, b_ref, cos_ref, sin_ref = (next(it) for _ in range(7))
    if up:
        wuk_ref, wuvt_ref, gkn_ref = next(it), next(it), next(it)
    for _ in range(3):
        next(it)
    ckv_o, kpe_o, logf_o = next(it), next(it), next(it)
    if up:
        kcat_o, vt_o, ckv_scr, kpe_scr = (next(it) for _ in range(4))

    @pl.when(pl.program_id(1) == 0)
    def _():
        z = _dot(h_ref[...], w6_ref[...])
        ckv = _rms(z[:, :kvl], gkv_ref[...])
        ckv_o[...] = ckv
        rot = _rotary(_rms(z[:, kvl:kvl + LANES], gk_ref[...]), cos_ref[...], sin_ref[...])
        kpe_o[...] = rot[:, :rope]
        logf_o[...] = _log_sigmoid(z[:, kvl + LANES:] + b_ref[...])[:, :fh]
        if up:
            ckv_scr[...] = ckv.astype(BF16)
            kpe_scr[...] = rot.astype(BF16)

    if up:
        ckv = ckv_scr[...]
        kn = _dot(ckv, wuk_ref[...])
        vt = _dot_nt(wuvt_ref[...], ckv)
        for hh in range(hg):
            knh = _rms(kn[:, hh * nope:(hh + 1) * nope], gkn_ref[...]).astype(BF16)
            kcat_o[hh] = jnp.concatenate([knh, kpe_scr[...]], axis=1)
            for c in range(vt.shape[1] // tk):
                vt_o[hh, c] = vt[hh * vdim:(hh + 1) * vdim, c * tk:(c + 1) * tk].astype(BF16)


def _mla_kv(h, w6, gkv, gk, b, cos, sin, wuk, wuvt, gkn, states, layer, *, up, heads, nope, vdim, rope, fh, tm, tk, hg):
    n, d = h.shape
    kvl = gkv.shape[2]
    nj = heads // hg if up else 1
    c2 = lambda shape: _layer_spec(shape, layer, lambda i, j: (0,) * len(shape))
    ins = [h, w6, gkv, gk, b, cos, sin]
    in_specs = [pl.BlockSpec((tm, d), lambda i, j: (i, 0)), c2(w6.shape[1:]), c2((1, kvl)), c2((1, LANES)),
                c2((1, LANES)), pl.BlockSpec((tm, LANES), lambda i, j: (i, 0)),
                pl.BlockSpec((tm, LANES), lambda i, j: (i, 0))]
    if up:
        ins += [wuk, wuvt, gkn]
        in_specs += [_layer_spec((kvl, hg * nope), layer, lambda i, j: (0, j)),
                     _layer_spec((hg * vdim, kvl), layer, lambda i, j: (j, 0)), c2((1, nope))]
    aliases = {len(ins) + k: k for k in range(3)}
    ins += list(states)
    in_specs += [pl.BlockSpec(memory_space=pl.ANY)] * 3
    out_shape = [jax.ShapeDtypeStruct(s.shape, F32) for s in states]
    out_specs = [pl.BlockSpec((None, tm, w), lambda i, j: (layer, i, 0)) for w in (kvl, rope, fh)]
    scratch = []
    if up:
        out_shape += [jax.ShapeDtypeStruct((heads, n, nope + LANES), BF16),
                      jax.ShapeDtypeStruct((heads, n // tk, vdim, tk), BF16)]
        out_specs += [pl.BlockSpec((hg, tm, nope + LANES), lambda i, j: (j, i, 0)),
                      pl.BlockSpec((hg, tm // tk, vdim, tk), lambda i, j: (j, i, 0, 0))]
        scratch = [pltpu.VMEM((tm, kvl), BF16), pltpu.VMEM((tm, LANES), BF16)]
    return pl.pallas_call(
        functools.partial(_mla_kv_kernel, up=up, hg=hg, kvl=kvl, rope=rope, fh=fh, nope=nope, vdim=vdim, tk=tk),
        grid=(n // tm, nj),
        in_specs=in_specs,
        out_specs=out_specs,
        out_shape=out_shape,
        input_output_aliases=aliases,
        scratch_shapes=scratch,
        compiler_params=_params("parallel", "arbitrary"),
        name="mla_kv",
    )(*ins)


def _cumsum_kernel(x_ref, tri_ref, o_ref):
    tri = tri_ref[...]
    carry = jnp.zeros((x_ref.shape[0], 1), F32)
    for c in range(x_ref.shape[1] // CUMSUM_BLOCK):
        sl = slice(c * CUMSUM_BLOCK, (c + 1) * CUMSUM_BLOCK)
        hi, mid, lo = _split3(x_ref[:, sl])
        blk = _dot(hi, tri) + _dot(mid, tri) + _dot(lo, tri) + carry
        o_ref[:, sl] = blk
        carry = blk[:, CUMSUM_BLOCK - 1:]


def _cumsum_time(x):
    b, hh, s = x.shape
    r = lax.broadcasted_iota(jnp.int32, (CUMSUM_BLOCK, CUMSUM_BLOCK), 0)
    c = lax.broadcasted_iota(jnp.int32, (CUMSUM_BLOCK, CUMSUM_BLOCK), 1)
    tri = (r <= c).astype(BF16)
    return pl.pallas_call(
        _cumsum_kernel,
        grid=(b,),
        in_specs=[pl.BlockSpec((None, hh, s), lambda i: (i, 0, 0)),
                  pl.BlockSpec((CUMSUM_BLOCK, CUMSUM_BLOCK), lambda i: (0, 0))],
        out_specs=pl.BlockSpec((None, hh, s), lambda i: (i, 0, 0)),
        out_shape=jax.ShapeDtypeStruct(x.shape, F32),
        compiler_params=_params("parallel"),
        name="cumsum_logf",
    )(x, tri)


def _attn_prompt_kernel(qt_ref, k_ref, vt_ref, o_ref, sa_scr, mxa_scr, sb_scr, mxb_scr, m_scr, l_scr, acc_scr,
                        *, hg, tq, chunk):
    i = pl.program_id(2)
    m_scr[...] = jnp.full_like(m_scr, MASKED)
    l_scr[...] = jnp.zeros_like(l_scr)
    acc_scr[...] = jnp.zeros_like(acc_scr)
    bufs = {"a": (sa_scr, mxa_scr), "b": (sb_scr, mxb_scr)}

    def stash(j, buf, diagonal=False):
        s_ref, mx_ref = bufs[buf]
        start = pl.multiple_of(j * tq, tq)
        for hh in range(hg):
            t = _dot(k_ref[hh, pl.ds(start, tq), :], qt_ref[hh])
            if diagonal:
                ks = lax.broadcasted_iota(jnp.int32, t.shape, 0)
                qs = lax.broadcasted_iota(jnp.int32, t.shape, 1)
                vis = (ks // chunk) <= (qs // chunk) if chunk > 1 else ks <= qs
                t = jnp.where(vis, t, MASKED)
            s_ref[hh] = t
            mx_ref[hh] = jnp.max(t, axis=0, keepdims=True)

    def absorb(j, buf):
        s_ref, mx_ref = bufs[buf]
        for hh in range(hg):
            m_old = m_scr[hh]
            m_new = jnp.maximum(m_old, mx_ref[hh])
            alpha = jnp.exp2(m_old - m_new)
            p = jnp.exp2(s_ref[hh] - m_new)
            l_scr[hh] = alpha * l_scr[hh] + jnp.sum(p, axis=0, keepdims=True)
            acc_scr[hh] = alpha * acc_scr[hh] + _dot(vt_ref[hh, j], p.astype(BF16))
            m_scr[hh] = m_new

    lead = (i + 1) % 2

    @pl.when((i >= 1) & (lead == 0))
    def _():
        stash(0, "a")

    @pl.when((i >= 1) & (lead == 1))
    def _():
        stash(0, "b")
        stash(1, "a")
        absorb(0, "b")

    def body(jj, carry):
        t0 = lead + 2 * jj
        stash(t0 + 1, "b")
        absorb(t0, "a")
        stash(t0 + 2, "a")
        absorb(t0 + 1, "b")
        return carry

    lax.fori_loop(0, (i - 1 - lead) // 2, body, 0)

    @pl.when(i >= 1)
    def _():
        stash(i, "b", diagonal=True)
        absorb(i - 1, "a")

    @pl.when(i == 0)
    def _():
        stash(i, "b", diagonal=True)

    absorb(i, "b")
    o_ref[...] = jnp.concatenate([(acc_scr[hh] / l_scr[hh]).T for hh in range(hg)], axis=1).astype(o_ref.dtype)


def _attn_prompt(qt, k, vt, *, batch, seq, tq, chunk, hg):
    heads, dk, n = qt.shape
    dv = vt.shape[2]
    nq = seq // tq
    return pl.pallas_call(
        functools.partial(_attn_prompt_kernel, hg=hg, tq=tq, chunk=chunk),
        grid=(batch, heads // hg, nq),
        in_specs=[pl.BlockSpec((hg, dk, tq), lambda b, g, i: (g, 0, b * nq + i)),
                  pl.BlockSpec((hg, seq, dk), lambda b, g, i: (g, b, 0)),
                  pl.BlockSpec((hg, nq, dv, tq), lambda b, g, i: (g, b, 0, 0))],
        out_specs=pl.BlockSpec((tq, hg * dv), lambda b, g, i: (b * nq + i, g)),
        out_shape=jax.ShapeDtypeStruct((n, heads * dv), BF16),
        scratch_shapes=[pltpu.VMEM((hg, tq, tq), F32), pltpu.VMEM((hg, 1, tq), F32),
                        pltpu.VMEM((hg, tq, tq), F32), pltpu.VMEM((hg, 1, tq), F32),
                        pltpu.VMEM((hg, 1, tq), F32), pltpu.VMEM((hg, 1, tq), F32), pltpu.VMEM((hg, dv, tq), F32)],
        compiler_params=_params("parallel", "parallel", "arbitrary"),
        name="attn_prompt_chunk%d" % chunk,
    )(qt, k, vt)


def _softmax_two_segments(s_c, s_n, v_c, v_n, transposed_v=False):
    m = jnp.maximum(jnp.max(s_c, axis=1, keepdims=True), jnp.max(s_n, axis=1, keepdims=True))
    p_c = jnp.exp(s_c - m)
    p_n = jnp.exp(s_n - m)
    l = jnp.sum(p_c, axis=1, keepdims=True) + jnp.sum(p_n, axis=1, keepdims=True)
    pv = _dot_nt if transposed_v else _dot
    return (pv(p_c.astype(BF16), v_c) + pv(p_n.astype(BF16), v_n)) / l


def _fox_sample_kernel(q_ref, kc_ref, vc_ref, kn_ref, vn_ref, fq_ref, fkc_ref, fkn_ref, o_ref, *, heads, hd, scale):
    t = q_ref.shape[0]
    rows = lax.broadcasted_iota(jnp.int32, (t, t), 0)
    cols = lax.broadcasted_iota(jnp.int32, (t, t), 1)
    outs = []
    for hh in range(heads):
        sl = slice(hh * hd, (hh + 1) * hd)
        q = q_ref[:, sl]
        fq = fq_ref[hh]
        rows_h = pl.ds(hh, kc_ref.shape[0] // heads, stride=heads)
        s_c = _dot_nt(q, kc_ref[rows_h, :].astype(BF16)) * scale + (fq - fkc_ref[hh])
        s_n = _dot_nt(q, kn_ref[:, sl].astype(BF16)) * scale + (fq - fkn_ref[hh])
        s_n = jnp.where(cols <= rows, s_n, MASKED)
        outs.append(_softmax_two_segments(s_c, s_n, vc_ref[rows_h, :].astype(BF16), vn_ref[:, sl].astype(BF16)))
    o_ref[...] = jnp.concatenate(outs, axis=1).astype(o_ref.dtype)


def _fox_sample(q, kc, vc, k_state, v_state, layer, fq, fkc, fkn):
    n, width = q.shape
    depth, b, s, heads, hd = kc.shape
    t = n // b
    kc, vc = (a.reshape(depth, b, s * heads, hd) for a in (kc, vc))
    cache = pl.BlockSpec((None, None, s * heads, hd), lambda i: (layer, i, 0, 0))
    new = pl.BlockSpec((None, t, width), lambda i: (layer, i, 0))
    return pl.pallas_call(
        functools.partial(_fox_sample_kernel, heads=heads, hd=hd, scale=float(hd) ** -0.5),
        grid=(b,),
        in_specs=[pl.BlockSpec((t, width), lambda i: (i, 0)), cache, cache, new, new,
                  pl.BlockSpec((None, heads, t, 1), lambda i: (i, 0, 0, 0)),
                  pl.BlockSpec((None, heads, 1, s), lambda i: (i, 0, 0, 0)),
                  pl.BlockSpec((None, heads, 1, t), lambda i: (i, 0, 0, 0))],
        out_specs=pl.BlockSpec((t, width), lambda i: (i, 0)),
        out_shape=jax.ShapeDtypeStruct((n, width), BF16),
        compiler_params=_params("parallel"),
        name="fox_sample",
    )(q, kc, vc, k_state, v_state, fq, fkc, fkn)


def _mla_sample_kernel(q_ref, cc_ref, pc_ref, cn_ref, pn_ref, wkt_ref, wuv_ref, gkn_ref, o_ref, cc_scr, pct_scr,
                       *, hg, nope, vdim, past, scale, chunk):
    t = q_ref.shape[1]
    rows = lax.broadcasted_iota(jnp.int32, (t, t), 0) + past
    cols = lax.broadcasted_iota(jnp.int32, (t, t), 1) + past
    vis = (cols // chunk) <= (rows // chunk)

    def rope_rows(p_ref):
        pe = p_ref[...]
        pad = jnp.zeros((pe.shape[0], LANES - pe.shape[1]), F32)
        return jnp.concatenate([pe, pad], axis=1).T.astype(BF16)

    @pl.when(pl.program_id(1) == 0)
    def _():
        cc_scr[...] = cc_ref[...].astype(BF16)
        pct_scr[...] = rope_rows(pc_ref)

    pnt = rope_rows(pn_ref)
    cn = cn_ref[...].astype(BF16)
    knt_c = _dot_nt(wkt_ref[...], cc_scr[...])
    knt_n = _dot_nt(wkt_ref[...], cn)
    g = gkn_ref[...]
    p_cs, p_ns, ls = [], [], []
    for hh in range(hg):
        q = q_ref[hh]

        def keys_t(knt, pet):
            kn = _rms(knt[hh * nope:(hh + 1) * nope, :], g, axis=0).astype(BF16)
            return jnp.concatenate([kn, pet], axis=0)

        s_c = _dot(q, keys_t(knt_c, pct_scr[...])) * scale
        s_n = jnp.where(vis, _dot(q, keys_t(knt_n, pnt)) * scale, MASKED)
        m = jnp.maximum(jnp.max(s_c, axis=1, keepdims=True), jnp.max(s_n, axis=1, keepdims=True))
        p_c = jnp.exp(s_c - m)
        p_n = jnp.exp(s_n - m)
        ls.append(jnp.sum(p_c, axis=1, keepdims=True) + jnp.sum(p_n, axis=1, keepdims=True))
        p_cs.append(p_c.astype(BF16))
        p_ns.append(p_n.astype(BF16))
    ctx = _dot(jnp.concatenate(p_cs, axis=0), cc_scr[...]) + _dot(jnp.concatenate(p_ns, axis=0), cn)
    outs = [_dot(ctx[hh * t:(hh + 1) * t, :].astype(BF16), wuv_ref[:, hh * vdim:(hh + 1) * vdim]) / ls[hh]
            for hh in range(hg)]
    o_ref[...] = jnp.concatenate(outs, axis=1).astype(o_ref.dtype)


def _mla_sample(q, ckv_cache, kpe_cache, ckv_state, kpe_state, layer, wukt, wuv, gkn, *, heads, nope, vdim, hg, chunk):
    _, n, dk = q.shape
    _, b, s, kvl = ckv_cache.shape
    rope = kpe_cache.shape[3]
    t = n // b
    return pl.pallas_call(
        functools.partial(_mla_sample_kernel, hg=hg, nope=nope, vdim=vdim, past=s,
                          scale=float(nope + rope) ** -0.5, chunk=chunk),
        grid=(b, heads // hg),
        in_specs=[pl.BlockSpec((hg, t, dk), lambda i, g: (g, i, 0)),
                  pl.BlockSpec((None, None, s, kvl), lambda i, g: (layer, i, 0, 0)),
                  pl.BlockSpec((None, None, s, rope), lambda i, g: (layer, i, 0, 0)),
                  pl.BlockSpec((None, t, kvl), lambda i, g: (layer, i, 0)),
                  pl.BlockSpec((None, t, rope), lambda i, g: (layer, i, 0)),
                  _layer_spec((hg * nope, kvl), layer, lambda i, g: (g, 0)),
                  _layer_spec((kvl, hg * vdim), layer, lambda i, g: (0, g)),
                  _layer_spec((nope, 1), layer, lambda i, g: (0, 0))],
        out_specs=pl.BlockSpec((t, hg * vdim), lambda i, g: (i, g)),
        out_shape=jax.ShapeDtypeStruct((n, heads * vdim), BF16),
        scratch_shapes=[pltpu.VMEM((s, kvl), BF16), pltpu.VMEM((LANES, s), BF16)],
        compiler_params=_params("parallel", "arbitrary"),
        name="mla_sample",
    )(q, ckv_cache, kpe_cache, ckv_state, kpe_state, wukt, wuv, gkn)


def _merge_kernel(x_ref, h_ref, of_ref, om_ref, wga_ref, wgb_ref, wa_ref, wb_ref, wo_ref, o_ref):
    j = pl.program_id(1)

    @pl.when(j == 0)
    def _():
        o_ref[...] = jnp.zeros_like(o_ref)

    h = h_ref[...]
    ga = jax.nn.sigmoid(_dot(h, wga_ref[...]))
    gb = jax.nn.sigmoid(_dot(h, wgb_ref[...]))
    m = ga * _dot(of_ref[...], wa_ref[...]) + gb * _dot(om_ref[...], wb_ref[...])
    o_ref[...] += _dot(m.astype(BF16), wo_ref[...])

    @pl.when(j == pl.num_programs(1) - 1)
    def _():
        o_ref[...] = x_ref[...] + o_ref[...]


def _merge(x, h, of, om, wga, wgb, wa, wb, wo, layer, *, tm, tn):
    n, d = x.shape
    row = lambda c: pl.BlockSpec((tm, c), lambda i, j: (i, 0))
    col = lambda r: _layer_spec((r, tn), layer, lambda i, j: (0, j))
    return pl.pallas_call(
        _merge_kernel,
        grid=(n // tm, d // tn),
        in_specs=[row(d), row(d), row(of.shape[1]), row(om.shape[1]),
                  col(d), col(d), col(of.shape[1]), col(om.shape[1]),
                  _layer_spec((tn, d), layer, lambda i, j: (j, 0))],
        out_specs=row(d),
        out_shape=jax.ShapeDtypeStruct((n, d), F32),
        compiler_params=_params("parallel", "arbitrary"),
        name="merge_out",
    )(x, h, of, om, wga, wgb, wa, wb, wo)


def _swap_halves(a):
    half = a.shape[-1] // 2
    return jnp.concatenate([a[..., half:], a[..., :half]], axis=-1)


def _rope_angles(pos, rope):
    half = rope // 2
    inv = ROPE_THETA ** (-jnp.arange(half, dtype=F32) / half)
    ang = pos.astype(F32)[:, None] * inv[None, :]
    return jnp.cos(ang), jnp.sin(ang)


def kernel(x_prompt, x_sample, cache_fox_k, cache_fox_v, cache_fox_logf, cache_mla_ckv, cache_mla_kpe, ffn1_norm, ffn1_w_gate, ffn1_w_up, ffn1_w_down, mix_norm, w_in, b_forget, fox_q_norm, fox_k_norm, mla_q_lat_norm, w_uq, mla_q_nope_norm, mla_q_rope_norm, mla_kv_lat_norm, w_ukv, mla_k_nope_norm, mla_k_rope_norm, w_branch_a, w_branch_b, w_out, ffn2_norm, ffn2_w_gate, ffn2_w_up, ffn2_w_down, post_norm):
    depth, d = ffn1_norm.shape
    bp, tp, _ = x_prompt.shape
    bs, ts, _ = x_sample.shape
    past, fh, hd = cache_fox_k.shape[2], cache_fox_k.shape[3], cache_fox_k.shape[4]
    fw = fh * hd
    ql, kvl = mla_q_lat_norm.shape[1], mla_kv_lat_norm.shape[1]
    nope, rope = mla_q_nope_norm.shape[1], mla_q_rope_norm.shape[1]
    mh = w_uq.shape[2] // (nope + rope)
    vdim = w_ukv.shape[2] // mh - nope
    assert nope == LANES and 2 * rope == LANES and hd == LANES and vdim == LANES
    np_, ns = bp * tp, bs * ts

    bf = lambda a: a.astype(BF16)
    tr = lambda a: jnp.swapaxes(a, 1, 2)
    o = [0]
    for wdt in (fw, fw, fw, fh, ql, kvl, rope, d, d):
        o.append(o[-1] + wdt)
    w_qt = bf(tr(w_in[:, :, o[0]:o[1]]))
    w_k, w_v = bf(w_in[:, :, o[1]:o[2]]), bf(w_in[:, :, o[2]:o[3]])
    w_f, w_cq, w_ckv, w_kpe = (w_in[:, :, o[k]:o[k + 1]] for k in range(3, 7))
    w_ga, w_gb = bf(w_in[:, :, o[7]:o[8]]), bf(w_in[:, :, o[8]:o[9]])
    w_cqt = bf(tr(w_cq))
    w6 = bf(jnp.concatenate([w_ckv, w_kpe, _swap_halves(w_kpe), w_f,
                             jnp.zeros((depth, d, LANES - fh), F32)], axis=2))
    b6 = jnp.concatenate([b_forget, jnp.zeros((depth, LANES - fh), F32)], axis=1)
    uq = w_uq.reshape(depth, ql, mh, nope + rope)
    wuqt = bf(tr(jnp.concatenate([uq, _swap_halves(uq[..., nope:])], axis=3).reshape(depth, ql, mh * (nope + LANES))))
    g_qh = jnp.concatenate([mla_q_nope_norm, mla_q_rope_norm, _swap_halves(mla_q_rope_norm)], axis=1)
    g_kr = jnp.concatenate([mla_k_rope_norm, _swap_halves(mla_k_rope_norm)], axis=1)
    ukv = w_ukv.reshape(depth, kvl, mh, nope + vdim)
    wuk = bf(ukv[..., :nope].reshape(depth, kvl, mh * nope))
    wuv = bf(ukv[..., nope:].reshape(depth, kvl, mh * vdim))
    wukt, wuvt = tr(wuk), tr(wuv)
    f1g, f1u, f1d = bf(ffn1_w_gate), bf(ffn1_w_up), bf(ffn1_w_down)
    f2g, f2u, f2d = bf(ffn2_w_gate), bf(ffn2_w_up), bf(ffn2_w_down)
    wa, wb, wo = bf(w_branch_a), bf(w_branch_b), bf(w_out)
    rowv = lambda a: a[:, None, :]
    colv = lambda a: a[:, :, None]

    def rope_tables(pos, reps):
        cos, sin = _rope_angles(pos, rope)
        pad = jnp.zeros((pos.shape[0], LANES - rope), F32)
        lanes = (jnp.concatenate([cos, cos, pad], axis=1), jnp.concatenate([-sin, sin, pad], axis=1))
        subl = (jnp.concatenate([cos, cos], axis=1).T, jnp.concatenate([-sin, sin], axis=1).T)
        return [jnp.tile(a, (reps, 1)) for a in lanes] + [jnp.tile(a, (1, reps)) for a in subl]

    rope_p = rope_tables(jnp.arange(tp, dtype=jnp.int32), bp)
    rope_s = rope_tables(past + jnp.arange(ts, dtype=jnp.int32), bs)

    def states(n):
        return [jnp.zeros((depth, n, w), F32) for w in (fw, fw, fh, kvl, rope)]

    st_p, st_s = states(np_), states(ns)

    tq = _tile(tp, ATTN_TILE)
    tm_p, tm_s = _tile(np_, 512), _tile(ns, 512)
    assert tm_p % tq == 0 and tp % tq == 0
    tf = _tile(f1g.shape[2], 512)
    tn = _tile(d, 512)
    s_pad = -(-(past + ts) // CUMSUM_BLOCK) * CUMSUM_BLOCK
    fox_scale, mla_scale = float(hd) ** -0.5, float(nope + rope) ** -0.5

    def tokens(x, st, l, *, prompt):
        tm = tm_p if prompt else tm_s
        cos, sin, cos_t, sin_t = rope_p if prompt else rope_s
        n = x.shape[0]
        tm_wide = _tile(n, 2 * tm)
        x1, h = _ffn(x, rowv(ffn1_norm), f1g, f1u, f1d, rowv(mix_norm), l, post=False, tm=tm, tf=tf)
        res = _mla_kv(h, w6, rowv(mla_kv_lat_norm), rowv(g_kr), rowv(b6), cos, sin, wuk, wuvt,
                      rowv(mla_k_nope_norm), (st[3], st[4], st[2]), l, up=prompt, heads=mh, nope=nope,
                      vdim=vdim, rope=rope, fh=fh, tm=tm_wide, tk=tq, hg=4)
        st[3], st[4], st[2] = res[:3]
        q_m = _mla_q(h, w_cqt, colv(mla_q_lat_norm), wuqt, colv(g_qh), cos_t, sin_t, l, heads=mh, nope=nope,
                     rope=rope, qscale=mla_scale * LOG2E if prompt else 1.0, tm=tm_wide, hg=4)
        kv = functools.partial(_fox_kv, heads=fh, hd=hd, prompt=prompt, tm=tm, tk=tq)
        if prompt:
            logf_t = jnp.swapaxes(st[2][l].reshape(bp, tp, fh), 1, 2)
            fcum_t = _cumsum_time(logf_t)
            fcum = jnp.swapaxes(fcum_t, 1, 2).reshape(n, fh)
            fcum_t = jnp.swapaxes(fcum_t, 0, 1).reshape(fh, n)
            q_f = _fox_q(h, w_qt, colv(fox_q_norm), fcum_t, l, heads=fh, hd=hd, prompt=True, tm=tm)
            st[0], k_f = kv(h, w_k, rowv(fox_k_norm), fcum, st[0], l, mode="k")
            st[1], vt_f = kv(h, w_v, None, None, st[1], l, mode="v")
            o_f = _attn_prompt(q_f, k_f, vt_f, batch=bp, seq=tp, tq=tq, chunk=1, hg=ATTN_HEADS)
            o_m = _attn_prompt(q_m, res[3], res[4], batch=bp, seq=tp, tq=tq, chunk=CHUNK, hg=ATTN_HEADS)
        else:
            q_f = _fox_q(h, w_qt, colv(fox_q_norm), None, l, heads=fh, hd=hd, prompt=False, tm=tm)
            q_f = jnp.transpose(q_f, (2, 0, 1)).reshape(n, fw)
            (st[0],) = kv(h, w_k, rowv(fox_k_norm), None, st[0], l, mode="k")
            (st[1],) = kv(h, w_v, None, None, st[1], l, mode="v")
            logf_all = jnp.concatenate([cache_fox_logf[l], st[2][l].reshape(bs, ts, fh),
                                        jnp.zeros((bs, s_pad - past - ts, fh), F32)], axis=1)
            fall = _cumsum_time(jnp.swapaxes(logf_all, 1, 2))
            fkc = fall[:, :, None, :past]
            fkn = fall[:, :, None, past:past + ts]
            fq = fall[:, :, past:past + ts, None]
            o_f = _fox_sample(q_f, cache_fox_k, cache_fox_v, st[0], st[1], l, fq, fkc, fkn)
            o_m = _mla_sample(jnp.swapaxes(q_m, 1, 2), cache_mla_ckv, cache_mla_kpe, st[3], st[4], l, wukt, wuv,
                              colv(mla_k_nope_norm), heads=mh, nope=nope, vdim=vdim, hg=min(8, mh), chunk=CHUNK)
        x2 = _merge(x1, h, o_f, o_m, w_ga, w_gb, wa, wb, wo, l, tm=tm, tn=tn)
        return _ffn(x2, rowv(ffn2_norm), f2g, f2u, f2d, rowv(post_norm), l, post=True, tm=tm, tf=tf)

    y_p, y_s = x_prompt.reshape(np_, d), x_sample.reshape(ns, d)
    for l in range(depth):
        y_p = tokens(y_p, st_p, l, prompt=True)
        y_s = tokens(y_s, st_s, l, prompt=False)

    def shaped(st, b, t):
        return (st[0].reshape(depth, b, t, fh, hd), st[1].reshape(depth, b, t, fh, hd), st[2].reshape(depth, b, t, fh),
                st[3].reshape(depth, b, t, kvl), st[4].reshape(depth, b, t, rope))

    return (y_p.reshape(bp, tp, d), y_s.reshape(bs, ts, d)) + shaped(st_p, bp, tp) + shaped(st_s, bs, ts)
```

```python
import functools
import math

import jax
import jax.numpy as jnp
from jax import lax
from jax.experimental import pallas as pl
from jax.experimental.pallas import tpu as pltpu

F32 = jnp.float32
BF16 = jnp.bfloat16

NORM_EPS = 1e-6
CHUNK = 64
ROPE_THETA = 10000.0
MASKED = -1e30
LOG2E = math.log2(math.e)

LANES = 128
MXU_DIM = 256
VMEM_LIMIT_BYTES = 56 * 1024 * 1024
CUMSUM_BLOCK = MXU_DIM
ATTN_TILE = 512
ATTN_HEADS = 4


def _params(*semantics):
    return pltpu.CompilerParams(dimension_semantics=semantics, vmem_limit_bytes=VMEM_LIMIT_BYTES)


def _tile(n, pref):
    t = min(n, pref)
    while n % t:
        t -= 8
    return t


def _rms(x, g, axis=-1):
    return x * lax.rsqrt(jnp.mean(x * x, axis=axis, keepdims=True) + NORM_EPS) * g


def _dot(a, b):
    return jnp.dot(a, b, preferred_element_type=F32)


def _dot_nt(a, b):
    return lax.dot_general(a, b, (((1,), (1,)), ((), ())), preferred_element_type=F32)


def _split3(x):
    hi = x.astype(BF16)
    r = x - hi.astype(F32)
    mid = r.astype(BF16)
    lo = (r - mid.astype(F32)).astype(BF16)
    return hi, mid, lo


def _layer_spec(shape, layer, index_map):
    return pl.BlockSpec((None,) + tuple(shape), lambda *ids: (layer,) + tuple(index_map(*ids)))


def _ffn_kernel(x_ref, g_ref, wg_ref, wu_ref, wd_ref, g2_ref, o_ref, *rest, post):
    if post:
        (h_scr,) = rest
    else:
        h2_ref, h_scr = rest
    j = pl.program_id(1)

    @pl.when(j == 0)
    def _():
        h_scr[...] = _rms(x_ref[...], g_ref[...]).astype(BF16)
        o_ref[...] = jnp.zeros_like(o_ref)

    h = h_scr[...]
    g = _dot(h, wg_ref[...])
    u = _dot(h, wu_ref[...])
    a = (g * jax.nn.sigmoid(g) * u).astype(BF16)
    o_ref[...] += _dot(a, wd_ref[...])

    @pl.when(j == pl.num_programs(1) - 1)
    def _():
        y = x_ref[...] + 0.5 * o_ref[...]
        if post:
            o_ref[...] = _rms(y, g2_ref[...])
        else:
            o_ref[...] = y
            h2_ref[...] = _rms(y, g2_ref[...]).astype(BF16)


def _ffn(x, g, wg, wu, wd, g2, layer, *, post, tm, tf):
    n, d = x.shape
    dff = wg.shape[2]
    row = pl.BlockSpec((tm, d), lambda i, j: (i, 0))
    vec = _layer_spec((1, d), layer, lambda i, j: (0, 0))
    out_shape = [jax.ShapeDtypeStruct((n, d), F32)]
    out_specs = [row]
    if not post:
        out_shape.append(jax.ShapeDtypeStruct((n, d), BF16))
        out_specs.append(row)
    res = pl.pallas_call(
        functools.partial(_ffn_kernel, post=post),
        grid=(n // tm, dff // tf),
        in_specs=[row, vec,
                  _layer_spec((d, tf), layer, lambda i, j: (0, j)),
                  _layer_spec((d, tf), layer, lambda i, j: (0, j)),
                  _layer_spec((tf, d), layer, lambda i, j: (j, 0)),
                  vec],
        out_specs=out_specs,
        out_shape=out_shape,
        scratch_shapes=[pltpu.VMEM((tm, d), BF16)],
        compiler_params=_params("parallel", "arbitrary"),
        name="ffn_post" if post else "ffn",
    )(x, g, wg, wu, wd, g2)
    return res[0] if post else res


def _bias_slots(f, axis, *, query):
    hi, mid, lo = (p.astype(F32) for p in _split3(f))
    shape = (LANES, f.shape[1]) if axis == 0 else (f.shape[0], LANES)
    slot = lax.broadcasted_iota(jnp.int32, shape, axis)
    if not query:
        hi, mid, lo = -hi, -mid, -lo
        slot = slot - 3
    vals = jnp.where(slot == 0, hi, jnp.where(slot == 1, mid, jnp.where(slot == 2, lo, 0.0)))
    ones = (slot >= 3) & (slot < 6) if query else (slot >= -3) & (slot < 0)
    return jnp.where(ones, 1.0, vals)


def _fox_q_kernel(*refs, heads, hd, with_bias, qscale):
    if with_bias:
        h_ref, w_ref, g_ref, f_ref, o_ref = refs
    else:
        h_ref, w_ref, g_ref, o_ref = refs
    zt = _dot_nt(w_ref[...], h_ref[...])
    g = jnp.broadcast_to(g_ref[...], (hd, zt.shape[1]))
    for hh in range(heads):
        qn = _rms(zt[hh * hd:(hh + 1) * hd, :], g, axis=0) * qscale
        if with_bias:
            bias = _bias_slots(f_ref[hh:hh + 1, :] * LOG2E, 0, query=True)
            o_ref[hh] = jnp.concatenate([qn, bias], axis=0).astype(BF16)
        else:
            o_ref[hh] = qn.astype(BF16)


def _fox_q(h, wqt, g, fcum_t, layer, *, heads, hd, prompt, tm):
    n, d = h.shape
    ins = [h, wqt, g]
    in_specs = [pl.BlockSpec((tm, d), lambda i: (i, 0)),
                _layer_spec((heads * hd, d), layer, lambda i: (0, 0)),
                _layer_spec((hd, 1), layer, lambda i: (0, 0))]
    if prompt:
        ins.append(fcum_t)
        in_specs.append(pl.BlockSpec((heads, tm), lambda i: (0, i)))
    dk = hd + LANES if prompt else hd
    return pl.pallas_call(
        functools.partial(_fox_q_kernel, heads=heads, hd=hd, with_bias=prompt,
                          qscale=float(hd) ** -0.5 * LOG2E if prompt else 1.0),
        grid=(n // tm,),
        in_specs=in_specs,
        out_specs=pl.BlockSpec((heads, dk, tm), lambda i: (0, 0, i)),
        out_shape=jax.ShapeDtypeStruct((heads, dk, n), BF16),
        compiler_params=_params("parallel"),
        name="fox_q",
    )(*ins)


def _fox_kv_kernel(*refs, mode, heads, hd, with_bias, tk):
    it = iter(refs)
    h_ref, w_ref = next(it), next(it)
    g_ref = next(it) if mode == "k" else None
    f_ref = next(it) if with_bias else None
    next(it)
    state_ref = next(it)
    o_ref = next(it, None)
    z = _dot(h_ref[...], w_ref[...])
    if with_bias:
        fs = f_ref[...] * LOG2E
    outs = []
    for hh in range(heads):
        zh = z[:, hh * hd:(hh + 1) * hd]
        if mode == "k":
            zh = _rms(zh, g_ref[...])
        outs.append(zh)
        if o_ref is None:
            continue
        if mode == "v":
            for c in range(zh.shape[0] // tk):
                o_ref[hh, c] = zh[c * tk:(c + 1) * tk, :].T.astype(BF16)
        else:
            o_ref[hh] = jnp.concatenate([zh, _bias_slots(fs[:, hh:hh + 1], 1, query=False)], axis=1).astype(BF16)
    state_ref[...] = jnp.concatenate(outs, axis=1)


def _fox_kv(h, w, g, fcum, state, layer, *, mode, heads, hd, prompt, tm, tk):
    n, d = h.shape
    width = heads * hd
    with_bias = prompt and mode == "k"
    row = lambda c: pl.BlockSpec((tm, c), lambda i: (i, 0))
    ins, in_specs = [h, w], [row(d), _layer_spec((d, width), layer, lambda i: (0, 0))]
    if mode == "k":
        ins.append(g)
        in_specs.append(_layer_spec((1, hd), layer, lambda i: (0, 0)))
    if with_bias:
        ins.append(fcum)
        in_specs.append(row(heads))
    aliases = {len(ins): 0}
    ins.append(state)
    in_specs.append(pl.BlockSpec(memory_space=pl.ANY))
    out_shape = [jax.ShapeDtypeStruct(state.shape, F32)]
    out_specs = [pl.BlockSpec((None, tm, width), lambda i: (layer, i, 0))]
    if prompt and mode == "v":
        out_shape.append(jax.ShapeDtypeStruct((heads, n // tk, hd, tk), BF16))
        out_specs.append(pl.BlockSpec((heads, tm // tk, hd, tk), lambda i: (0, i, 0, 0)))
    elif with_bias:
        out_shape.append(jax.ShapeDtypeStruct((heads, n, hd + LANES), BF16))
        out_specs.append(pl.BlockSpec((heads, tm, hd + LANES), lambda i: (0, i, 0)))
    return pl.pallas_call(
        functools.partial(_fox_kv_kernel, mode=mode, heads=heads, hd=hd, with_bias=with_bias, tk=tk),
        grid=(n // tm,),
        in_specs=in_specs,
        out_specs=out_specs,
        out_shape=out_shape,
        input_output_aliases=aliases,
        compiler_params=_params("parallel"),
        name="fox_" + mode,
    )(*ins)


def _mla_q_kernel(h_ref, wcq_ref, gq_ref, wuq_ref, gh_ref, cos_ref, sin_ref, o_ref, cq_scr, *, hg, nope, rope, qscale):
    @pl.when(pl.program_id(1) == 0)
    def _():
        cq_scr[...] = _rms(_dot_nt(wcq_ref[...], h_ref[...]), gq_ref[...], axis=0).astype(BF16)

    zt = _dot(wuq_ref[...], cq_scr[...])
    tm = zt.shape[1]
    gh = gh_ref[...]
    g_n = jnp.broadcast_to(gh[:nope], (nope, tm))
    g_r = jnp.broadcast_to(gh[nope:nope + rope], (rope, tm))
    g_s = jnp.broadcast_to(gh[nope + rope:], (rope, tm))
    cos, sin = cos_ref[...], sin_ref[...]
    pad = jnp.zeros((LANES - rope, tm), F32)
    per = nope + LANES
    for hh in range(hg):
        base = hh * per
        qn = _rms(zt[base:base + nope, :], g_n, axis=0)
        r = zt[base + nope:base + nope + rope, :]
        inv = lax.rsqrt(jnp.mean(r * r, axis=0, keepdims=True) + NORM_EPS)
        rot = (r * inv * g_r) * cos + (zt[base + nope + rope:base + per, :] * inv * g_s) * sin
        o_ref[hh] = jnp.concatenate([qn * qscale, rot * qscale, pad], axis=0).astype(BF16)


def _mla_q(h, wcqt, gq, wuqt, gh, cos_t, sin_t, layer, *, heads, nope, rope, qscale, tm, hg):
    n, d = h.shape
    ql = wcqt.shape[1]
    per = nope + LANES
    return pl.pallas_call(
        functools.partial(_mla_q_kernel, hg=hg, nope=nope, rope=rope, qscale=qscale),
        grid=(n // tm, heads // hg),
        in_specs=[pl.BlockSpec((tm, d), lambda i, j: (i, 0)),
                  _layer_spec((ql, d), layer, lambda i, j: (0, 0)),
                  _layer_spec((ql, 1), layer, lambda i, j: (0, 0)),
                  _layer_spec((hg * per, ql), layer, lambda i, j: (j, 0)),
                  _layer_spec((per, 1), layer, lambda i, j: (0, 0)),
                  pl.BlockSpec((rope, tm), lambda i, j: (0, i)),
                  pl.BlockSpec((rope, tm), lambda i, j: (0, i))],
        out_specs=pl.BlockSpec((hg, per, tm), lambda i, j: (j, 0, i)),
        out_shape=jax.ShapeDtypeStruct((heads, per, n), BF16),
        scratch_shapes=[pltpu.VMEM((ql, tm), BF16)],
        compiler_params=_params("parallel", "arbitrary"),
        name="mla_q",
    )(h, wcqt, gq, wuqt, gh, cos_t, sin_t)


def _rotary(y, cos, sin):
    return y * cos + pltpu.roll(y, LANES // 2, 1) * sin


def _log_sigmoid(x):
    return jnp.minimum(x, 0.0) - jnp.log1p(jnp.exp(-jnp.abs(x)))


def _mla_kv_kernel(*refs, up, hg, kvl, rope, fh, nope, vdim, tk):
    it = iter(refs)
    h_ref, w6_ref, gkv_ref, gk_ref, b_ref, cos_ref, sin_ref = (next(it) for _ in range(7))
    if up:
        wuk_ref, wuvt_ref, gkn_ref = next(it), next(it), next(it)
    for _ in range(3):
        next(it)
    ckv_o, kpe_o, logf_o = next(it), next(it), next(it)
    if up:
        kcat_o, vt_o, ckv_scr, kpe_scr = (next(it) for _ in range(4))

    @pl.when(pl.program_id(1) == 0)
    def _():
        z = _dot(h_ref[...], w6_ref[...])
        ckv = _rms(z[:, :kvl], gkv_ref[...])
        ckv_o[...] = ckv
        rot = _rotary(_rms(z[:, kvl:kvl + LANES], gk_ref[...]), cos_ref[...], sin_ref[...])
        kpe_o[...] = rot[:, :rope]
        logf_o[...] = _log_sigmoid(z[:, kvl + LANES:] + b_ref[...])[:, :fh]
        if up:
            ckv_scr[...] = ckv.astype(BF16)
            kpe_scr[...] = rot.astype(BF16)

    if up:
        ckv = ckv_scr[...]
        kn = _dot(ckv, wuk_ref[...])
        vt = _dot_nt(wuvt_ref[...], ckv)
        for hh in range(hg):
            knh = _rms(kn[:, hh * nope:(hh + 1) * nope], gkn_ref[...]).astype(BF16)
            kcat_o[hh] = jnp.concatenate([knh, kpe_scr[...]], axis=1)
            for c in range(vt.shape[1] // tk):
                vt_o[hh, c] = vt[hh * vdim:(hh + 1) * vdim, c * tk:(c + 1) * tk].astype(BF16)


def _mla_kv(h, w6, gkv, gk, b, cos, sin, wuk, wuvt, gkn, states, layer, *, up, heads, nope, vdim, rope, fh, tm, tk, hg):
    n, d = h.shape
    kvl = gkv.shape[2]
    nj = heads // hg if up else 1
    c2 = lambda shape: _layer_spec(shape, layer, lambda i, j: (0,) * len(shape))
    ins = [h, w6, gkv, gk, b, cos, sin]
    in_specs = [pl.BlockSpec((tm, d), lambda i, j: (i, 0)), c2(w6.shape[1:]), c2((1, kvl)), c2((1, LANES)),
                c2((1, LANES)), pl.BlockSpec((tm, LANES), lambda i, j: (i, 0)),
                pl.BlockSpec((tm, LANES), lambda i, j: (i, 0))]
    if up:
        ins += [wuk, wuvt, gkn]
        in_specs += [_layer_spec((kvl, hg * nope), layer, lambda i, j: (0, j)),
                     _layer_spec((hg * vdim, kvl), layer, lambda i, j: (j, 0)), c2((1, nope))]
    aliases = {len(ins) + k: k for k in range(3)}
    ins += list(states)
    in_specs += [pl.BlockSpec(memory_space=pl.ANY)] * 3
    out_shape = [jax.ShapeDtypeStruct(s.shape, F32) for s in states]
    out_specs = [pl.BlockSpec((None, tm, w), lambda i, j: (layer, i, 0)) for w in (kvl, rope, fh)]
    scratch = []
    if up:
        out_shape += [jax.ShapeDtypeStruct((heads, n, nope + LANES), BF16),
                      jax.ShapeDtypeStruct((heads, n // tk, vdim, tk), BF16)]
        out_specs += [pl.BlockSpec((hg, tm, nope + LANES), lambda i, j: (j, i, 0)),
                      pl.BlockSpec((hg, tm // tk, vdim, tk), lambda i, j: (j, i, 0, 0))]
        scratch = [pltpu.VMEM((tm, kvl), BF16), pltpu.VMEM((tm, LANES), BF16)]
    return pl.pallas_call(
        functools.partial(_mla_kv_kernel, up=up, hg=hg, kvl=kvl, rope=rope, fh=fh, nope=nope, vdim=vdim, tk=tk),
        grid=(n // tm, nj),
        in_specs=in_specs,
        out_specs=out_specs,
        out_shape=out_shape,
        input_output_aliases=aliases,
        scratch_shapes=scratch,
        compiler_params=_params("parallel", "arbitrary"),
        name="mla_kv",
    )(*ins)


def _cumsum_kernel(x_ref, tri_ref, o_ref):
    tri = tri_ref[...]
    carry = jnp.zeros((x_ref.shape[0], 1), F32)
    for c in range(x_ref.shape[1] // CUMSUM_BLOCK):
        sl = slice(c * CUMSUM_BLOCK, (c + 1) * CUMSUM_BLOCK)
        hi, mid, lo = _split3(x_ref[:, sl])
        blk = _dot(hi, tri) + _dot(mid, tri) + _dot(lo, tri) + carry
        o_ref[:, sl] = blk
        carry = blk[:, CUMSUM_BLOCK - 1:]


def _cumsum_time(x):
    b, hh, s = x.shape
    r = lax.broadcasted_iota(jnp.int32, (CUMSUM_BLOCK, CUMSUM_BLOCK), 0)
    c = lax.broadcasted_iota(jnp.int32, (CUMSUM_BLOCK, CUMSUM_BLOCK), 1)
    tri = (r <= c).astype(BF16)
    return pl.pallas_call(
        _cumsum_kernel,
        grid=(b,),
        in_specs=[pl.BlockSpec((None, hh, s), lambda i: (i, 0, 0)),
                  pl.BlockSpec((CUMSUM_BLOCK, CUMSUM_BLOCK), lambda i: (0, 0))],
        out_specs=pl.BlockSpec((None, hh, s), lambda i: (i, 0, 0)),
        out_shape=jax.ShapeDtypeStruct(x.shape, F32),
        compiler_params=_params("parallel"),
        name="cumsum_logf",
    )(x, tri)


def _attn_prompt_kernel(qt_ref, k_ref, vt_ref, o_ref, sa_scr, mxa_scr, sb_scr, mxb_scr, m_scr, l_scr, acc_scr,
                        *, hg, tq, chunk):
    i = pl.program_id(2)
    m_scr[...] = jnp.full_like(m_scr, MASKED)
    l_scr[...] = jnp.zeros_like(l_scr)
    acc_scr[...] = jnp.zeros_like(acc_scr)
    bufs = {"a": (sa_scr, mxa_scr), "b": (sb_scr, mxb_scr)}

    def stash(j, buf, diagonal=False):
        s_ref, mx_ref = bufs[buf]
        start = pl.multiple_of(j * tq, tq)
        for hh in range(hg):
            t = _dot(k_ref[hh, pl.ds(start, tq), :], qt_ref[hh])
            if diagonal:
                ks = lax.broadcasted_iota(jnp.int32, t.shape, 0)
                qs = lax.broadcasted_iota(jnp.int32, t.shape, 1)
                vis = (ks // chunk) <= (qs // chunk) if chunk > 1 else ks <= qs
                t = jnp.where(vis, t, MASKED)
            s_ref[hh] = t
            mx_ref[hh] = jnp.max(t, axis=0, keepdims=True)

    def absorb(j, buf):
        s_ref, mx_ref = bufs[buf]
        for hh in range(hg):
            m_old = m_scr[hh]
            m_new = jnp.maximum(m_old, mx_ref[hh])
            alpha = jnp.exp2(m_old - m_new)
            p = jnp.exp2(s_ref[hh] - m_new)
            l_scr[hh] = alpha * l_scr[hh] + jnp.sum(p, axis=0, keepdims=True)
            acc_scr[hh] = alpha * acc_scr[hh] + _dot(vt_ref[hh, j], p.astype(BF16))
            m_scr[hh] = m_new

    lead = (i + 1) % 2

    @pl.when((i >= 1) & (lead == 0))
    def _():
        stash(0, "a")

    @pl.when((i >= 1) & (lead == 1))
    def _():
        stash(0, "b")
        stash(1, "a")
        absorb(0, "b")

    def body(jj, carry):
        t0 = lead + 2 * jj
        stash(t0 + 1, "b")
        absorb(t0, "a")
        stash(t0 + 2, "a")
        absorb(t0 + 1, "b")
        return carry

    lax.fori_loop(0, (i - 1 - lead) // 2, body, 0)

    @pl.when(i >= 1)
    def _():
        stash(i, "b", diagonal=True)
        absorb(i - 1, "a")

    @pl.when(i == 0)
    def _():
        stash(i, "b", diagonal=True)

    absorb(i, "b")
    o_ref[...] = jnp.concatenate([(acc_scr[hh] / l_scr[hh]).T for hh in range(hg)], axis=1).astype(o_ref.dtype)


def _attn_prompt(qt, k, vt, *, batch, seq, tq, chunk, hg):
    heads, dk, n = qt.shape
    dv = vt.shape[2]
    nq = seq // tq
    return pl.pallas_call(
        functools.partial(_attn_prompt_kernel, hg=hg, tq=tq, chunk=chunk),
        grid=(batch, heads // hg, nq),
        in_specs=[pl.BlockSpec((hg, dk, tq), lambda b, g, i: (g, 0, b * nq + i)),
                  pl.BlockSpec((hg, seq, dk), lambda b, g, i: (g, b, 0)),
                  pl.BlockSpec((hg, nq, dv, tq), lambda b, g, i: (g, b, 0, 0))],
        out_specs=pl.BlockSpec((tq, hg * dv), lambda b, g, i: (b * nq + i, g)),
        out_shape=jax.ShapeDtypeStruct((n, heads * dv), BF16),
        scratch_shapes=[pltpu.VMEM((hg, tq, tq), F32), pltpu.VMEM((hg, 1, tq), F32),
                        pltpu.VMEM((hg, tq, tq), F32), pltpu.VMEM((hg, 1, tq), F32),
                        pltpu.VMEM((hg, 1, tq), F32), pltpu.VMEM((hg, 1, tq), F32), pltpu.VMEM((hg, dv, tq), F32)],
        compiler_params=_params("parallel", "parallel", "arbitrary"),
        name="attn_prompt_chunk%d" % chunk,
    )(qt, k, vt)


def _softmax_two_segments(s_c, s_n, v_c, v_n, transposed_v=False):
    m = jnp.maximum(jnp.max(s_c, axis=1, keepdims=True), jnp.max(s_n, axis=1, keepdims=True))
    p_c = jnp.exp(s_c - m)
    p_n = jnp.exp(s_n - m)
    l = jnp.sum(p_c, axis=1, keepdims=True) + jnp.sum(p_n, axis=1, keepdims=True)
    pv = _dot_nt if transposed_v else _dot
    return (pv(p_c.astype(BF16), v_c) + pv(p_n.astype(BF16), v_n)) / l


def _fox_sample_kernel(q_ref, kc_ref, vc_ref, kn_ref, vn_ref, fq_ref, fkc_ref, fkn_ref, o_ref, *, heads, hd, scale):
    t = q_ref.shape[0]
    rows = lax.broadcasted_iota(jnp.int32, (t, t), 0)
    cols = lax.broadcasted_iota(jnp.int32, (t, t), 1)
    outs = []
    for hh in range(heads):
        sl = slice(hh * hd, (hh + 1) * hd)
        q = q_ref[:, sl]
        fq = fq_ref[hh]
        rows_h = pl.ds(hh, kc_ref.shape[0] // heads, stride=heads)
        s_c = _dot_nt(q, kc_ref[rows_h, :].astype(BF16)) * scale + (fq - fkc_ref[hh])
        s_n = _dot_nt(q, kn_ref[:, sl].astype(BF16)) * scale + (fq - fkn_ref[hh])
        s_n = jnp.where(cols <= rows, s_n, MASKED)
        outs.append(_softmax_two_segments(s_c, s_n, vc_ref[rows_h, :].astype(BF16), vn_ref[:, sl].astype(BF16)))
    o_ref[...] = jnp.concatenate(outs, axis=1).astype(o_ref.dtype)


def _fox_sample(q, kc, vc, k_state, v_state, layer, fq, fkc, fkn):
    n, width = q.shape
    depth, b, s, heads, hd = kc.shape
    t = n // b
    kc, vc = (a.reshape(depth, b, s * heads, hd) for a in (kc, vc))
    cache = pl.BlockSpec((None, None, s * heads, hd), lambda i: (layer, i, 0, 0))
    new = pl.BlockSpec((None, t, width), lambda i: (layer, i, 0))
    return pl.pallas_call(
        functools.partial(_fox_sample_kernel, heads=heads, hd=hd, scale=float(hd) ** -0.5),
        grid=(b,),
        in_specs=[pl.BlockSpec((t, width), lambda i: (i, 0)), cache, cache, new, new,
                  pl.BlockSpec((None, heads, t, 1), lambda i: (i, 0, 0, 0)),
                  pl.BlockSpec((None, heads, 1, s), lambda i: (i, 0, 0, 0)),
                  pl.BlockSpec((None, heads, 1, t), lambda i: (i, 0, 0, 0))],
        out_specs=pl.BlockSpec((t, width), lambda i: (i, 0)),
        out_shape=jax.ShapeDtypeStruct((n, width), BF16),
        compiler_params=_params("parallel"),
        name="fox_sample",
    )(q, kc, vc, k_state, v_state, fq, fkc, fkn)


def _mla_sample_kernel(q_ref, cc_ref, pc_ref, cn_ref, pn_ref, wkt_ref, wuv_ref, gkn_ref, o_ref, cc_scr, pct_scr,
                       *, hg, nope, vdim, past, scale, chunk):
    t = q_ref.shape[1]
    rows = lax.broadcasted_iota(jnp.int32, (t, t), 0) + past
    cols = lax.broadcasted_iota(jnp.int32, (t, t), 1) + past
    vis = (cols // chunk) <= (rows // chunk)

    def rope_rows(p_ref):
        pe = p_ref[...]
        pad = jnp.zeros((pe.shape[0], LANES - pe.shape[1]), F32)
        return jnp.concatenate([pe, pad], axis=1).T.astype(BF16)

    @pl.when(pl.program_id(1) == 0)
    def _():
        cc_scr[...] = cc_ref[...].astype(BF16)
        pct_scr[...] = rope_rows(pc_ref)

    pnt = rope_rows(pn_ref)
    cn = cn_ref[...].astype(BF16)
    knt_c = _dot_nt(wkt_ref[...], cc_scr[...])
    knt_n = _dot_nt(wkt_ref[...], cn)
    g = gkn_ref[...]
    p_cs, p_ns, ls = [], [], []
    for hh in range(hg):
        q = q_ref[hh]

        def keys_t(knt, pet):
            kn = _rms(knt[hh * nope:(hh + 1) * nope, :], g, axis=0).astype(BF16)
            return jnp.concatenate([kn, pet], axis=0)

        s_c = _dot(q, keys_t(knt_c, pct_scr[...])) * scale
        s_n = jnp.where(vis, _dot(q, keys_t(knt_n, pnt)) * scale, MASKED)
        m = jnp.maximum(jnp.max(s_c, axis=1, keepdims=True), jnp.max(s_n, axis=1, keepdims=True))
        p_c = jnp.exp(s_c - m)
        p_n = jnp.exp(s_n - m)
        ls.append(jnp.sum(p_c, axis=1, keepdims=True) + jnp.sum(p_n, axis=1, keepdims=True))
        p_cs.append(p_c.astype(BF16))
        p_ns.append(p_n.astype(BF16))
    ctx = _dot(jnp.concatenate(p_cs, axis=0), cc_scr[...]) + _dot(jnp.concatenate(p_ns, axis=0), cn)
    outs = [_dot(ctx[hh * t:(hh + 1) * t, :].astype(BF16), wuv_ref[:, hh * vdim:(hh + 1) * vdim]) / ls[hh]
            for hh in range(hg)]
    o_ref[...] = jnp.concatenate(outs, axis=1).astype(o_ref.dtype)


def _mla_sample(q, ckv_cache, kpe_cache, ckv_state, kpe_state, layer, wukt, wuv, gkn, *, heads, nope, vdim, hg, chunk):
    _, n, dk = q.shape
    _, b, s, kvl = ckv_cache.shape
    rope = kpe_cache.shape[3]
    t = n // b
    return pl.pallas_call(
        functools.partial(_mla_sample_kernel, hg=hg, nope=nope, vdim=vdim, past=s,
                          scale=float(nope + rope) ** -0.5, chunk=chunk),
        grid=(b, heads // hg),
        in_specs=[pl.BlockSpec((hg, t, dk), lambda i, g: (g, i, 0)),
                  pl.BlockSpec((None, None, s, kvl), lambda i, g: (layer, i, 0, 0)),
                  pl.BlockSpec((None, None, s, rope), lambda i, g: (layer, i, 0, 0)),
                  pl.BlockSpec((None, t, kvl), lambda i, g: (layer, i, 0)),
                  pl.BlockSpec((None, t, rope), lambda i, g: (layer, i, 0)),
                  _layer_spec((hg * nope, kvl), layer, lambda i, g: (g, 0)),
                  _layer_spec((kvl, hg * vdim), layer, lambda i, g: (0, g)),
                  _layer_spec((nope, 1), layer, lambda i, g: (0, 0))],
        out_specs=pl.BlockSpec((t, hg * vdim), lambda i, g: (i, g)),
        out_shape=jax.ShapeDtypeStruct((n, heads * vdim), BF16),
        scratch_shapes=[pltpu.VMEM((s, kvl), BF16), pltpu.VMEM((LANES, s), BF16)],
        compiler_params=_params("parallel", "arbitrary"),
        name="mla_sample",
    )(q, ckv_cache, kpe_cache, ckv_state, kpe_state, wukt, wuv, gkn)


def _merge_kernel(x_ref, h_ref, of_ref, om_ref, wga_ref, wgb_ref, wa_ref, wb_ref, wo_ref, o_ref):
    j = pl.program_id(1)

    @pl.when(j == 0)
    def _():
        o_ref[...] = jnp.zeros_like(o_ref)

    h = h_ref[...]
    ga = jax.nn.sigmoid(_dot(h, wga_ref[...]))
    gb = jax.nn.sigmoid(_dot(h, wgb_ref[...]))
    m = ga * _dot(of_ref[...], wa_ref[...]) + gb * _dot(om_ref[...], wb_ref[...])
    o_ref[...] += _dot(m.astype(BF16), wo_ref[...])

    @pl.when(j == pl.num_programs(1) - 1)
    def _():
        o_ref[...] = x_ref[...] + o_ref[...]


def _merge(x, h, of, om, wga, wgb, wa, wb, wo, layer, *, tm, tn):
    n, d = x.shape
    row = lambda c: pl.BlockSpec((tm, c), lambda i, j: (i, 0))
    col = lambda r: _layer_spec((r, tn), layer, lambda i, j: (0, j))
    return pl.pallas_call(
        _merge_kernel,
        grid=(n // tm, d // tn),
        in_specs=[row(d), row(d), row(of.shape[1]), row(om.shape[1]),
                  col(d), col(d), col(of.shape[1]), col(om.shape[1]),
                  _layer_spec((tn, d), layer, lambda i, j: (j, 0))],
        out_specs=row(d),
        out_shape=jax.ShapeDtypeStruct((n, d), F32),
        compiler_params=_params("parallel", "arbitrary"),
        name="merge_out",
    )(x, h, of, om, wga, wgb, wa, wb, wo)


def _swap_halves(a):
    half = a.shape[-1] // 2
    return jnp.concatenate([a[..., half:], a[..., :half]], axis=-1)


def _rope_angles(pos, rope):
    half = rope // 2
    inv = ROPE_THETA ** (-jnp.arange(half, dtype=F32) / half)
    ang = pos.astype(F32)[:, None] * inv[None, :]
    return jnp.cos(ang), jnp.sin(ang)


def kernel(x_prompt, x_sample, cache_fox_k, cache_fox_v, cache_fox_logf, cache_mla_ckv, cache_mla_kpe, ffn1_norm, ffn1_w_gate, ffn1_w_up, ffn1_w_down, mix_norm, w_in, b_forget, fox_q_norm, fox_k_norm, mla_q_lat_norm, w_uq, mla_q_nope_norm, mla_q_rope_norm, mla_kv_lat_norm, w_ukv, mla_k_nope_norm, mla_k_rope_norm, w_branch_a, w_branch_b, w_out, ffn2_norm, ffn2_w_gate, ffn2_w_up, ffn2_w_down, post_norm):
    depth, d = ffn1_norm.shape
    bp, tp, _ = x_prompt.shape
    bs, ts, _ = x_sample.shape
    past, fh, hd = cache_fox_k.shape[2], cache_fox_k.shape[3], cache_fox_k.shape[4]
    fw = fh * hd
    ql, kvl = mla_q_lat_norm.shape[1], mla_kv_lat_norm.shape[1]
    nope, rope = mla_q_nope_norm.shape[1], mla_q_rope_norm.shape[1]
    mh = w_uq.shape[2] // (nope + rope)
    vdim = w_ukv.shape[2] // mh - nope
    assert nope == LANES and 2 * rope == LANES and hd == LANES and vdim == LANES
    np_, ns = bp * tp, bs * ts

    bf = lambda a: a.astype(BF16)
    tr = lambda a: jnp.swapaxes(a, 1, 2)
    o = [0]
    for wdt in (fw, fw, fw, fh, ql, kvl, rope, d, d):
        o.append(o[-1] + wdt)
    w_qt = bf(tr(w_in[:, :, o[0]:o[1]]))
    w_k, w_v = bf(w_in[:, :, o[1]:o[2]]), bf(w_in[:, :, o[2]:o[3]])
    w_f, w_cq, w_ckv, w_kpe = (w_in[:, :, o[k]:o[k + 1]] for k in range(3, 7))
    w_ga, w_gb = bf(w_in[:, :, o[7]:o[8]]), bf(w_in[:, :, o[8]:o[9]])
    w_cqt = bf(tr(w_cq))
    w6 = bf(jnp.concatenate([w_ckv, w_kpe, _swap_halves(w_kpe), w_f,
                             jnp.zeros((depth, d, LANES - fh), F32)], axis=2))
    b6 = jnp.concatenate([b_forget, jnp.zeros((depth, LANES - fh), F32)], axis=1)
    uq = w_uq.reshape(depth, ql, mh, nope + rope)
    wuqt = bf(tr(jnp.concatenate([uq, _swap_halves(uq[..., nope:])], axis=3).reshape(depth, ql, mh * (nope + LANES))))
    g_qh = jnp.concatenate([mla_q_nope_norm, mla_q_rope_norm, _swap_halves(mla_q_rope_norm)], axis=1)
    g_kr = jnp.concatenate([mla_k_rope_norm, _swap_halves(mla_k_rope_norm)], axis=1)
    ukv = w_ukv.reshape(depth, kvl, mh, nope + vdim)
    wuk = bf(ukv[..., :nope].reshape(depth, kvl, mh * nope))
    wuv = bf(ukv[..., nope:].reshape(depth, kvl, mh * vdim))
    wukt, wuvt = tr(wuk), tr(wuv)
    f1g, f1u, f1d = bf(ffn1_w_gate), bf(ffn1_w_up), bf(ffn1_w_down)
    f2g, f2u, f2d = bf(ffn2_w_gate), bf(ffn2_w_up), bf(ffn2_w_down)
    wa, wb, wo = bf(w_branch_a), bf(w_branch_b), bf(w_out)
    rowv = lambda a: a[:, None, :]
    colv = lambda a: a[:, :, None]

    def rope_tables(pos, reps):
        cos, sin = _rope_angles(pos, rope)
        pad = jnp.zeros((pos.shape[0], LANES - rope), F32)
        lanes = (jnp.concatenate([cos, cos, pad], axis=1), jnp.concatenate([-sin, sin, pad], axis=1))
        subl = (jnp.concatenate([cos, cos], axis=1).T, jnp.concatenate([-sin, sin], axis=1).T)
        return [jnp.tile(a, (reps, 1)) for a in lanes] + [jnp.tile(a, (1, reps)) for a in subl]

    rope_p = rope_tables(jnp.arange(tp, dtype=jnp.int32), bp)
    rope_s = rope_tables(past + jnp.arange(ts, dtype=jnp.int32), bs)

    def states(n):
        return [jnp.zeros((depth, n, w), F32) for w in (fw, fw, fh, kvl, rope)]

    st_p, st_s = states(np_), states(ns)

    tq = _tile(tp, ATTN_TILE)
    tm_p, tm_s = _tile(np_, 512), _tile(ns, 512)
    assert tm_p % tq == 0 and tp % tq == 0
    tf = _tile(f1g.shape[2], 512)
    tn = _tile(d, 512)
    s_pad = -(-(past + ts) // CUMSUM_BLOCK) * CUMSUM_BLOCK
    fox_scale, mla_scale = float(hd) ** -0.5, float(nope + rope) ** -0.5

    def tokens(x, st, l, *, prompt):
        tm = tm_p if prompt else tm_s
        cos, sin, cos_t, sin_t = rope_p if prompt else rope_s
        n = x.shape[0]
        tm_wide = _tile(n, 2 * tm)
        hg_wide = min(8, mh)
        x1, h = _ffn(x, rowv(ffn1_norm), f1g, f1u, f1d, rowv(mix_norm), l, post=False, tm=tm, tf=tf)
        res = _mla_kv(h, w6, rowv(mla_kv_lat_norm), rowv(g_kr), rowv(b6), cos, sin, wuk, wuvt,
                      rowv(mla_k_nope_norm), (st[3], st[4], st[2]), l, up=prompt, heads=mh, nope=nope,
                      vdim=vdim, rope=rope, fh=fh, tm=tm_wide, tk=tq, hg=hg_wide)
        st[3], st[4], st[2] = res[:3]
        q_m = _mla_q(h, w_cqt, colv(mla_q_lat_norm), wuqt, colv(g_qh), cos_t, sin_t, l, heads=mh, nope=nope,
                     rope=rope, qscale=mla_scale * LOG2E if prompt else 1.0, tm=tm_wide, hg=hg_wide)
        kv = functools.partial(_fox_kv, heads=fh, hd=hd, prompt=prompt, tm=tm_wide, tk=tq)
        if prompt:
            logf_t = jnp.swapaxes(st[2][l].reshape(bp, tp, fh), 1, 2)
            fcum_t = _cumsum_time(logf_t)
            fcum = jnp.swapaxes(fcum_t, 1, 2).reshape(n, fh)
            fcum_t = jnp.swapaxes(fcum_t, 0, 1).reshape(fh, n)
            q_f = _fox_q(h, w_qt, colv(fox_q_norm), fcum_t, l, heads=fh, hd=hd, prompt=True, tm=tm_wide)
            st[0], k_f = kv(h, w_k, rowv(fox_k_norm), fcum, st[0], l, mode="k")
            st[1], vt_f = kv(h, w_v, None, None, st[1], l, mode="v")
            o_f = _attn_prompt(q_f, k_f, vt_f, batch=bp, seq=tp, tq=tq, chunk=1, hg=ATTN_HEADS)
            o_m = _attn_prompt(q_m, res[3], res[4], batch=bp, seq=tp, tq=tq, chunk=CHUNK, hg=ATTN_HEADS)
        else:
            q_f = _fox_q(h, w_qt, colv(fox_q_norm), None, l, heads=fh, hd=hd, prompt=False, tm=tm_wide)
            q_f = jnp.transpose(q_f, (2, 0, 1)).reshape(n, fw)
            (st[0],) = kv(h, w_k, rowv(fox_k_norm), None, st[0], l, mode="k")
            (st[1],) = kv(h, w_v, None, None, st[1], l, mode="v")
            logf_all = jnp.concatenate([cache_fox_logf[l], st[2][l].reshape(bs, ts, fh),
                                        jnp.zeros((bs, s_pad - past - ts, fh), F32)], axis=1)
            fall = _cumsum_time(jnp.swapaxes(logf_all, 1, 2))
            fkc = fall[:, :, None, :past]
            fkn = fall[:, :, None, past:past + ts]
            fq = fall[:, :, past:past + ts, None]
            o_f = _fox_sample(q_f, cache_fox_k, cache_fox_v, st[0], st[1], l, fq, fkc, fkn)
            o_m = _mla_sample(jnp.swapaxes(q_m, 1, 2), cache_mla_ckv, cache_mla_kpe, st[3], st[4], l, wukt, wuv,
                              colv(mla_k_nope_norm), heads=mh, nope=nope, vdim=vdim, hg=min(8, mh), chunk=CHUNK)
        x2 = _merge(x1, h, o_f, o_m, w_ga, w_gb, wa, wb, wo, l, tm=tm, tn=tn)
        return _ffn(x2, rowv(ffn2_norm), f2g, f2u, f2d, rowv(post_norm), l, post=True, tm=tm, tf=tf)

    y_p, y_s = x_prompt.reshape(np_, d), x_sample.reshape(ns, d)
    for l in range(depth):
        y_p = tokens(y_p, st_p, l, prompt=True)
        y_s = tokens(y_s, st_s, l, prompt=False)

    def shaped(st, b, t):
        return (st[0].reshape(depth, b, t, fh, hd), st[1].reshape(depth, b, t, fh, hd), st[2].reshape(depth, b, t, fh),
                st[3].reshape(depth, b, t, kvl), st[4].reshape(depth, b, t, rope))

    return (y_p.reshape(bp, tp, d), y_s.reshape(bs, ts, d)) + shaped(st_p, bp, tp) + shaped(st_s, bs, ts)
```
